```python
import jax, jax.numpy as jnp
from jax import lax
import numpy as np

D_MODEL = 1024
BATCH = 4
SEQ = 4096
DEPTH = 1

EPS = 1e-6
ROPE_THETA = 10000.0
BLOCK = 128
NEG = -1e30

SWA_HEADS = 8
SWA_KV_HEADS = 2
SWA_HEAD_DIM = D_MODEL // 16
SWA_WINDOW = 128
SWA_WIDTH = SWA_HEADS * SWA_HEAD_DIM

MLA_HEADS = 8
MLA_NOPE_DIM = D_MODEL // 16
MLA_ROPE_DIM = D_MODEL // 32
MLA_V_DIM = D_MODEL // 16
MLA_Q_RANK = 3 * D_MODEL // 16
MLA_KV_RANK = D_MODEL // 8
MLA_WIDTH = MLA_HEADS * MLA_V_DIM

MIX_WIDTH = SWA_WIDTH + MLA_WIDTH
IN_SIZES = (SWA_WIDTH, SWA_KV_HEADS * SWA_HEAD_DIM, SWA_KV_HEADS * SWA_HEAD_DIM,
            MLA_Q_RANK, MLA_KV_RANK, MLA_ROPE_DIM)
IN_WIDTH = sum(IN_SIZES)
IN_SPLITS = [int(v) for v in np.cumsum(IN_SIZES)[:-1]]

N_GROUPS = 4
EXPERTS_PER_GROUP = 8
N_EXPERTS = N_GROUPS * EXPERTS_PER_GROUP
TOP_K = 2
D_EXPERT = D_MODEL // 4

N_MOD = 6

kernel_name = "hymba_swa_mla_hiermoe_adaln_encoder"


def rmsnorm(x, g):
    xf = x.astype(jnp.float32)
    y = xf * lax.rsqrt(jnp.mean(xf * xf, axis=-1, keepdims=True) + EPS)
    return (y * g.astype(jnp.float32)).astype(x.dtype)


def rope_tables(seq, dim):
    inv = 1.0 / (ROPE_THETA ** (jnp.arange(0, dim, 2, dtype=jnp.float32) / dim))
    ang = jnp.arange(seq, dtype=jnp.float32)[:, None] * inv[None, :]
    return jnp.cos(ang), jnp.sin(ang)


def apply_rope(x, cos, sin):
    half = x.shape[-1] // 2
    xf = x.astype(jnp.float32)
    x1, x2 = xf[..., :half], xf[..., half:]
    return jnp.concatenate([x1 * cos - x2 * sin, x2 * cos + x1 * sin], axis=-1).astype(x.dtype)


def windowed_gqa(q, k, v, sink, cos, sin):
    B, S = q.shape[0], q.shape[1]
    nb = S // BLOCK
    G = SWA_HEADS // SWA_KV_HEADS
    q = apply_rope(q, cos[:, None, :], sin[:, None, :])
    k = apply_rope(k, cos[:, None, :], sin[:, None, :])
    qb = q.reshape(B, nb, BLOCK, SWA_KV_HEADS, G, SWA_HEAD_DIM)
    pad = ((0, 0), (BLOCK, BLOCK), (0, 0), (0, 0))
    kp = jnp.pad(k, pad).reshape(B, nb + 2, BLOCK, SWA_KV_HEADS, SWA_HEAD_DIM)
    vp = jnp.pad(v, pad).reshape(B, nb + 2, BLOCK, SWA_KV_HEADS, SWA_HEAD_DIM)
    kb = jnp.concatenate([kp[:, :-2], kp[:, 1:-1], kp[:, 2:]], axis=2)
    vb = jnp.concatenate([vp[:, :-2], vp[:, 1:-1], vp[:, 2:]], axis=2)
    r = jnp.arange(BLOCK)[:, None]
    j = jnp.arange(3 * BLOCK)[None, :]
    in_win = jnp.abs(j - BLOCK - r) <= SWA_WINDOW
    kpos = jnp.arange(nb)[:, None] * BLOCK - BLOCK + jnp.arange(3 * BLOCK)[None, :]
    valid = (kpos >= 0) & (kpos < S)
    mask = in_win[None, :, :] & valid[:, None, :]
    scale = SWA_HEAD_DIM ** -0.5
    s = jnp.einsum('bnqkgd,bnjkd->bnkgqj', qb, kb).astype(jnp.float32) * scale
    s = jnp.where(mask[None, :, None, None, :, :], s, NEG)
    sink_f = sink.astype(jnp.float32).reshape(SWA_KV_HEADS, G)[None, None, :, :, None]
    m = jnp.maximum(jnp.max(s, axis=-1), sink_f)
    p = jnp.exp(s - m[..., None])
    denom = jnp.sum(p, axis=-1) + jnp.exp(sink_f - m)
    p = (p / denom[..., None]).astype(v.dtype)
    o = jnp.einsum('bnkgqj,bnjkd->bnqkgd', p, vb)
    return o.reshape(B, S, SWA_HEADS * SWA_HEAD_DIM)


def latent_attention(c_q, c_kv, k_rope, g_q, w_uq, g_kv, w_ukv, cos, sin):
    B, S = c_q.shape[0], c_q.shape[1]
    nb = S // BLOCK
    q = (rmsnorm(c_q, g_q) @ w_uq).reshape(B, S, MLA_HEADS, MLA_NOPE_DIM + MLA_ROPE_DIM)
    qn, qr = q[..., :MLA_NOPE_DIM], q[..., MLA_NOPE_DIM:]
    qr = apply_rope(qr, cos[:, None, :], sin[:, None, :])
    kr = apply_rope(k_rope, cos, sin)
    kv = (rmsnorm(c_kv, g_kv) @ w_ukv).reshape(B, S, MLA_HEADS, MLA_NOPE_DIM + MLA_V_DIM)
    kn, v = kv[..., :MLA_NOPE_DIM], kv[..., MLA_NOPE_DIM:]
    scale = (MLA_NOPE_DIM + MLA_ROPE_DIM) ** -0.5
    qn_b = qn.reshape(B, nb, BLOCK, MLA_HEADS, MLA_NOPE_DIM).transpose(1, 0, 2, 3, 4)
    qr_b = qr.reshape(B, nb, BLOCK, MLA_HEADS, MLA_ROPE_DIM).transpose(1, 0, 2, 3, 4)

    def attend(blk):
        qn_i, qr_i = blk
        s = (jnp.einsum('bqhd,bkhd->bhqk', qn_i, kn)
             + jnp.einsum('bqhr,bkr->bhqk', qr_i, kr)).astype(jnp.float32) * scale
        p = jax.nn.softmax(s, axis=-1).astype(v.dtype)
        return jnp.einsum('bhqk,bkhd->bqhd', p, v)

    o = lax.map(attend, (qn_b, qr_b))
    return o.transpose(1, 0, 2, 3, 4).reshape(B, S, MLA_HEADS * MLA_V_DIM)


def hierarchical_moe(h, w_rg, b_rg, w_re, b_re, w_gate, w_up, w_down):
    B, S, Dm = h.shape
    t = h.reshape(B * S, Dm)
    g_prob = jax.nn.softmax((t @ w_rg + b_rg).astype(jnp.float32), axis=-1)
    g_w, g_idx = lax.top_k(g_prob, 1)
    e_logit = (t @ w_re + b_re).astype(jnp.float32).reshape(-1, N_GROUPS, EXPERTS_PER_GROUP)
    g_onehot = jax.nn.one_hot(g_idx[:, 0], N_GROUPS, dtype=jnp.float32)
    e_logit = jnp.einsum('tge,tg->te', e_logit, g_onehot)
    e_prob = jax.nn.softmax(e_logit, axis=-1)
    e_w, e_idx = lax.top_k(e_prob, TOP_K)
    e_w = e_w / jnp.sum(e_w, axis=-1, keepdims=True)
    weights = g_w * e_w
    expert_id = g_idx * EXPERTS_PER_GROUP + e_idx
    combine = jnp.sum(jax.nn.one_hot(expert_id, N_EXPERTS, dtype=jnp.float32)
                      * weights[..., None], axis=1)
    hidden = (jax.nn.silu(jnp.einsum('td,edf->tef', t, w_gate))
              * jnp.einsum('td,edf->tef', t, w_up) * combine[..., None].astype(t.dtype))
    out = jnp.einsum('tef,efd->td', hidden, w_down)
    return out.reshape(B, S, Dm)


def setup_inputs(seed: int = 0) -> dict:
    key = jax.random.key(seed)
    ks = jax.random.split(key, 24)
    f32 = jnp.float32
    L = DEPTH

    def nrm(k, shape, fan_in, mult=1.0):
        return jax.random.normal(k, shape, f32) * (mult * fan_in ** -0.5)

    def gain(k, shape):
        return 1.0 + 0.02 * jax.random.normal(k, shape, f32)

    return {
        "x": jax.random.normal(ks[0], (BATCH, SEQ, D_MODEL), f32),
        "c": jax.random.normal(ks[1], (BATCH, D_MODEL), f32),
        "w_ada": nrm(ks[2], (L, D_MODEL, N_MOD * D_MODEL), D_MODEL, 0.5),
        "b_ada": 0.01 * jax.random.normal(ks[3], (L, N_MOD * D_MODEL), f32),
        "g_norm1": gain(ks[4], (L, D_MODEL)),
        "w_in": nrm(ks[5], (L, D_MODEL, IN_WIDTH), D_MODEL),
        "g_q_lora": gain(ks[6], (L, MLA_Q_RANK)),
        "w_uq": nrm(ks[7], (L, MLA_Q_RANK, MLA_HEADS * (MLA_NOPE_DIM + MLA_ROPE_DIM)), MLA_Q_RANK),
        "g_kv_lora": gain(ks[8], (L, MLA_KV_RANK)),
        "w_ukv": nrm(ks[9], (L, MLA_KV_RANK, MLA_HEADS * (MLA_NOPE_DIM + MLA_V_DIM)), MLA_KV_RANK),
        "sink": jax.random.normal(ks[10], (L, SWA_HEADS), f32),
        "g_out_swa": gain(ks[11], (L, SWA_WIDTH)),
        "g_out_mla": gain(ks[12], (L, MLA_WIDTH)),
        "w_out": nrm(ks[13], (L, MIX_WIDTH, D_MODEL), MIX_WIDTH),
        "g_norm2": gain(ks[14], (L, D_MODEL)),
        "w_router_group": nrm(ks[15], (L, D_MODEL, N_GROUPS), D_MODEL),
        "b_router_group": 0.01 * jax.random.normal(ks[16], (L, N_GROUPS), f32),
        "w_router_expert": nrm(ks[17], (L, D_MODEL, N_EXPERTS), D_MODEL),
        "b_router_expert": 0.01 * jax.random.normal(ks[18], (L, N_EXPERTS), f32),
        "w_exp_gate": nrm(ks[19], (L, N_EXPERTS, D_MODEL, D_EXPERT), D_MODEL),
        "w_exp_up": nrm(ks[20], (L, N_EXPERTS, D_MODEL, D_EXPERT), D_MODEL),
        "w_exp_down": nrm(ks[21], (L, N_EXPERTS, D_EXPERT, D_MODEL), D_EXPERT),
        "g_final": gain(ks[22], (D_MODEL,)),
    }


def reference(x, c, w_ada, b_ada, g_norm1, w_in, g_q_lora, w_uq, g_kv_lora, w_ukv, sink,
              g_out_swa, g_out_mla, w_out, g_norm2, w_router_group, b_router_group,
              w_router_expert, b_router_expert, w_exp_gate, w_exp_up, w_exp_down, g_final):
    B, S, _ = x.shape
    cos_a, sin_a = rope_tables(S, SWA_HEAD_DIM)
    cos_b, sin_b = rope_tables(S, MLA_ROPE_DIM)
    for l in range(DEPTH):
        mod = jax.nn.silu(c) @ w_ada[l] + b_ada[l]
        sh1, sc1, gt1, sh2, sc2, gt2 = [m[:, None, :] for m in jnp.split(mod, N_MOD, axis=-1)]

        h = rmsnorm(x, g_norm1[l]) * (1.0 + sc1) + sh1
        proj = h @ w_in[l]
        qa, ka, va, cq, ckv, kr = jnp.split(proj, IN_SPLITS, axis=-1)
        qa = qa.reshape(B, S, SWA_HEADS, SWA_HEAD_DIM)
        ka = ka.reshape(B, S, SWA_KV_HEADS, SWA_HEAD_DIM)
        va = va.reshape(B, S, SWA_KV_HEADS, SWA_HEAD_DIM)
        o_a = windowed_gqa(qa, ka, va, sink[l], cos_a, sin_a)
        o_b = latent_attention(cq, ckv, kr, g_q_lora[l], w_uq[l], g_kv_lora[l], w_ukv[l],
                               cos_b, sin_b)
        mix = jnp.concatenate([rmsnorm(o_a, g_out_swa[l]), rmsnorm(o_b, g_out_mla[l])], axis=-1)
        x = x + gt1 * (mix @ w_out[l])

        h = rmsnorm(x, g_norm2[l]) * (1.0 + sc2) + sh2
        x = x + gt2 * hierarchical_moe(h, w_router_group[l], b_router_group[l],
                                       w_router_expert[l], b_router_expert[l],
                                       w_exp_gate[l], w_exp_up[l], w_exp_down[l])
    return rmsnorm(x, g_final)
```

```python
import functools

import jax
import jax.numpy as jnp
import numpy as np
from jax import lax
from jax.experimental import pallas as pl
from jax.experimental.pallas import tpu as pltpu

F32 = jnp.float32
BF16 = jnp.bfloat16

D_MODEL = 1024
EPS = 1e-6
ROPE_THETA = 10000.0
NEG = -1e30
LANES = 128

BLOCK = 128
SWA_HEADS = 8
SWA_KV_HEADS = 2
SWA_HEAD_DIM = 64
SWA_WIDTH = SWA_HEADS * SWA_HEAD_DIM

MLA_HEADS = 8
MLA_NOPE = 64
MLA_ROPE = 32
MLA_V = 64
MLA_Q_RANK = 192
MLA_Q_RANK_PAD = 256
MLA_KV_RANK = 128
MLA_WIDTH = MLA_HEADS * MLA_V

N_GROUPS = 4
EXPERTS_PER_GROUP = 8
N_EXPERTS = 32
D_EXPERT = 256
N_MOD = 6

COL_QA = 0
COL_KA = 512
COL_VA = 640
COL_CKV = 768
COL_CQ = 896
COL_KR = 1152
IN_COLS = 1280

ROUTE_ROWS = 48

VMEM_LIMIT = 56 * 1024 * 1024

NT_DIMS = (((1,), (1,)), ((), ()))


def _dot(a, b):
    return jnp.dot(a, b, preferred_element_type=F32)


def _dot_nt(a, b):
    return lax.dot_general(a, b, NT_DIMS, preferred_element_type=F32)


def _split_bf16(a):
    hi = a.astype(BF16)
    lo = (a - hi.astype(F32)).astype(BF16)
    return hi, lo


def _params(sem):
    return pltpu.CompilerParams(dimension_semantics=sem, vmem_limit_bytes=VMEM_LIMIT)


def _ada_kernel(c_ref, w_ref, b_ref, o_ref):
    c = c_ref[...]
    a = c * (1.0 / (1.0 + jnp.exp(-c)))
    a_hi, a_lo = _split_bf16(a)
    w_hi, w_lo = _split_bf16(w_ref[...])
    o_ref[...] = _dot(a_hi, w_hi) + _dot(a_lo, w_hi) + _dot(a_hi, w_lo) + b_ref[...]


def _ada(c8, w_ada, b_ada):
    n = w_ada.shape[1]
    tn = 1024
    return pl.pallas_call(
        _ada_kernel,
        grid=(n // tn,),
        in_specs=[pl.BlockSpec((8, D_MODEL), lambda j: (0, 0)),
                  pl.BlockSpec((D_MODEL, tn), lambda j: (0, j)),
                  pl.BlockSpec((1, tn), lambda j: (0, j))],
        out_specs=pl.BlockSpec((8, tn), lambda j: (0, j)),
        out_shape=jax.ShapeDtypeStruct((8, n), F32),
        compiler_params=_params(("arbitrary",)),
        name="ada_mod",
    )(c8, w_ada, b_ada)


def _rope_slab(x, c, s1, s2, sh):
    return x * c + pltpu.roll(x, sh, 1) * s1 + pltpu.roll(x, LANES - sh, 1) * s2


def _proj_kernel(x_ref, mod_ref, g1_ref, tab_ref, win_ref, gq_ref, gkv_ref, wuq_ref, wk_ref, wvt_ref,
                 qa_ref, ka_ref, va_ref, qm_ref, km_ref, vt_ref):
    x = x_ref[0]
    ms = jnp.mean(x * x, axis=-1, keepdims=True)
    h = x * lax.rsqrt(ms + EPS) * g1_ref[...]
    h = h * (1.0 + mod_ref[0, 1:2, :]) + mod_ref[0, 0:1, :]
    proj = _dot(h.astype(BF16), win_ref[...])

    ca, sa1, sa2 = tab_ref[0], tab_ref[1], tab_ref[2]
    for s in range(SWA_WIDTH // LANES):
        q = proj[:, COL_QA + s * LANES:COL_QA + (s + 1) * LANES]
        qa_ref[0, :, s * LANES:(s + 1) * LANES] = _rope_slab(q, ca, sa1, sa2, 32).astype(BF16)
    ka_ref[0] = _rope_slab(proj[:, COL_KA:COL_KA + LANES], ca, sa1, sa2, 32).astype(BF16)
    va_ref[0] = proj[:, COL_VA:COL_VA + LANES].astype(BF16)

    ckv = proj[:, COL_CKV:COL_CKV + MLA_KV_RANK]
    ckvn = (ckv * lax.rsqrt(jnp.mean(ckv * ckv, axis=-1, keepdims=True) + EPS) * gkv_ref[...]).astype(BF16)
    cq = proj[:, COL_CQ:COL_CQ + MLA_Q_RANK_PAD]
    cq_ms = jnp.sum(cq * cq, axis=-1, keepdims=True) * (1.0 / MLA_Q_RANK)
    cqn = (cq * lax.rsqrt(cq_ms + EPS) * gq_ref[...]).astype(BF16)

    kr = _rope_slab(proj[:, COL_KR:COL_KR + LANES], tab_ref[6], tab_ref[7], tab_ref[8], 16)
    qm = _dot(cqn, wuq_ref[...])
    kn = _dot(ckvn, wk_ref[...])
    cq_t, sq1, sq2 = tab_ref[3], tab_ref[4], tab_ref[5]
    for hd in range(MLA_HEADS):
        sl = slice(hd * LANES, (hd + 1) * LANES)
        qm_ref[0, hd] = _rope_slab(qm[:, sl], cq_t, sq1, sq2, 16).astype(BF16)
        km_ref[0, hd] = (kn[:, sl] + kr).astype(BF16)
    vt_ref[0] = _dot_nt(wvt_ref[...], ckvn).astype(BF16)


def _proj(x, mod3, g1, tabs, w_in, gq, gkv, w_uq, w_k, w_vt, tm):
    b, s, _ = x.shape
    full = lambda shp: pl.BlockSpec(shp, lambda i, j: (0,) * len(shp))
    return pl.pallas_call(
        _proj_kernel,
        grid=(s // tm, b),
        in_specs=[pl.BlockSpec((1, tm, D_MODEL), lambda i, j: (j, i, 0)),
                  pl.BlockSpec((1, N_MOD, D_MODEL), lambda i, j: (j, 0, 0)),
                  full((1, D_MODEL)),
                  pl.BlockSpec((9, tm, LANES), lambda i, j: (0, i, 0)),
                  full((D_MODEL, IN_COLS)),
                  full((1, MLA_Q_RANK_PAD)),
                  full((1, MLA_KV_RANK)),
                  full((MLA_Q_RANK_PAD, MLA_HEADS * LANES)),
                  full((MLA_KV_RANK, MLA_HEADS * LANES)),
                  full((MLA_WIDTH, MLA_KV_RANK))],
        out_specs=[pl.BlockSpec((1, tm, SWA_WIDTH), lambda i, j: (j, i, 0)),
                   pl.BlockSpec((1, tm, LANES), lambda i, j: (j, i, 0)),
                   pl.BlockSpec((1, tm, LANES), lambda i, j: (j, i, 0)),
                   pl.BlockSpec((1, MLA_HEADS, tm, LANES), lambda i, j: (j, 0, i, 0)),
                   pl.BlockSpec((1, MLA_HEADS, tm, LANES), lambda i, j: (j, 0, i, 0)),
                   pl.BlockSpec((1, MLA_WIDTH, tm), lambda i, j: (j, 0, i))],
        out_shape=[jax.ShapeDtypeStruct((b, s, SWA_WIDTH), BF16),
                   jax.ShapeDtypeStruct((b, s, LANES), BF16),
                   jax.ShapeDtypeStruct((b, s, LANES), BF16),
                   jax.ShapeDtypeStruct((b, MLA_HEADS, s, LANES), BF16),
                   jax.ShapeDtypeStruct((b, MLA_HEADS, s, LANES), BF16),
                   jax.ShapeDtypeStruct((b, MLA_WIDTH, s), BF16)],
        compiler_params=_params(("arbitrary", "arbitrary")),
        name="norm1_proj",
    )(x, mod3, g1, tabs, w_in, gq, gkv, w_uq, w_k, w_vt)


def _swa_kernel(sink_ref, q_ref, kp_ref, kc_ref, kn_ref, vp_ref, vc_ref, vn_ref, g_ref, o_ref, *, nb):
    n = pl.program_id(1)
    k = jnp.concatenate([kp_ref[0], kc_ref[0], kn_ref[0]], axis=0)
    v = jnp.concatenate([vp_ref[0], vc_ref[0], vn_ref[0]], axis=0)
    k_sw = pltpu.roll(k, SWA_HEAD_DIM, 1)
    v_sw = pltpu.roll(v, SWA_HEAD_DIM, 1)
    lane = lax.broadcasted_iota(jnp.int32, k.shape, 1)
    lo = lane < SWA_HEAD_DIM
    zero = jnp.zeros_like(k)
    k_var = ((jnp.where(lo, k, zero), jnp.where(lo, zero, k_sw)),
             (jnp.where(lo, k_sw, zero), jnp.where(lo, zero, k)))
    v_var = ((jnp.where(lo, v, zero), jnp.where(lo, zero, v_sw)),
             (jnp.where(lo, v_sw, zero), jnp.where(lo, zero, v)))

    r = lax.broadcasted_iota(jnp.int32, (BLOCK, 3 * BLOCK), 0)
    j = lax.broadcasted_iota(jnp.int32, (BLOCK, 3 * BLOCK), 1)
    mask = (j >= r) & (j <= r + 2 * BLOCK)
    mask = mask & ((j >= BLOCK) | (n > 0)) & ((j < 2 * BLOCK) | (n < nb - 1))

    slabs = []
    ssq = jnp.zeros((BLOCK, 1), F32)
    group = SWA_HEADS // SWA_KV_HEADS
    for s in range(SWA_WIDTH // LANES):
        q = q_ref[0, :, s * LANES:(s + 1) * LANES]
        o_slab = jnp.zeros((BLOCK, LANES), F32)
        for half in range(2):
            head = 2 * s + half
            kv = head // group
            sc = _dot_nt(q, k_var[kv][half])
            sc = jnp.where(mask, sc, NEG)
            snk = sink_ref[0, head]
            m = jnp.maximum(jnp.max(sc, axis=-1, keepdims=True), snk)
            p = jnp.exp(sc - m)
            denom = jnp.sum(p, axis=-1, keepdims=True) + jnp.exp(snk - m)
            o_slab = o_slab + _dot(p.astype(BF16), v_var[kv][half]) * (1.0 / denom)
        slabs.append(o_slab)
        ssq = ssq + jnp.sum(o_slab * o_slab, axis=-1, keepdims=True)
    scale = lax.rsqrt(ssq * (1.0 / SWA_WIDTH) + EPS)
    for s, o_slab in enumerate(slabs):
        sl = slice(s * LANES, (s + 1) * LANES)
        o_ref[0, :, sl] = (o_slab * scale * g_ref[:, sl]).astype(BF16)


def _swa(sink, qa, ka, va, g_out):
    b, s, _ = qa.shape
    nb = s // BLOCK
    kv_spec = lambda f: pl.BlockSpec((1, BLOCK, LANES), f)
    prev = lambda i, n: (i, jnp.maximum(n - 1, 0), 0)
    cur = lambda i, n: (i, n, 0)
    nxt = lambda i, n: (i, jnp.minimum(n + 1, nb - 1), 0)
    return pl.pallas_call(
        functools.partial(_swa_kernel, nb=nb),
        grid=(b, nb),
        in_specs=[pl.BlockSpec(memory_space=pltpu.SMEM),
                  pl.BlockSpec((1, BLOCK, SWA_WIDTH), cur),
                  kv_spec(prev), kv_spec(cur), kv_spec(nxt),
                  kv_spec(prev), kv_spec(cur), kv_spec(nxt),
                  pl.BlockSpec((1, SWA_WIDTH), lambda i, n: (0, 0))],
        out_specs=pl.BlockSpec((1, BLOCK, SWA_WIDTH), cur),
        out_shape=jax.ShapeDtypeStruct((b, s, SWA_WIDTH), BF16),
        compiler_params=_params(("arbitrary", "arbitrary")),
        name="swa_attn",
    )(sink, qa, ka, ka, ka, va, va, va, g_out)


def _mla_kernel(q_ref, k_ref, vt_ref, o_ref, *, tk):
    tq = q_ref.shape[2]
    s_len = k_ref.shape[2]
    ones = jnp.ones((16, tk), BF16)
    outs = []
    for hd in range(2):
        q = q_ref[0, hd]

        def body(c, carry):
            m, acc = carry
            off = pl.multiple_of(c * tk, tk)
            k = k_ref[0, hd, pl.ds(off, tk), :]
            st = _dot_nt(k, q)
            m_new = jnp.maximum(m, jnp.max(st, axis=0, keepdims=True))
            alpha = jnp.exp(m - m_new)
            p = jnp.exp(st - m_new).astype(BF16)
            v = vt_ref[0, hd * MLA_V:(hd + 1) * MLA_V, pl.ds(off, tk)]
            vext = jnp.concatenate([v, ones], axis=0)
            acc = acc * alpha + _dot(vext, p)
            return m_new, acc

        m0 = jnp.full((1, tq), NEG, F32)
        acc0 = jnp.zeros((MLA_V + 16, tq), F32)
        _, acc = lax.fori_loop(0, s_len // tk, body, (m0, acc0))
        outs.append(acc[:MLA_V] * (1.0 / acc[MLA_V:MLA_V + 1]))
    o_ref[0] = jnp.concatenate(outs, axis=0).T


def _mla(qm, km, vt, tq, tk):
    b, h, s, _ = qm.shape
    return pl.pallas_call(
        functools.partial(_mla_kernel, tk=tk),
        grid=(b, h // 2, s // tq),
        in_specs=[pl.BlockSpec((1, 2, tq, LANES), lambda i, p, t: (i, p, t, 0)),
                  pl.BlockSpec((1, 2, s, LANES), lambda i, p, t: (i, p, 0, 0)),
                  pl.BlockSpec((1, 2 * MLA_V, s), lambda i, p, t: (i, p, 0))],
        out_specs=pl.BlockSpec((1, tq, LANES), lambda i, p, t: (i, t, p)),
        out_shape=jax.ShapeDtypeStruct((b, s, MLA_WIDTH), F32),
        compiler_params=_params(("arbitrary", "arbitrary", "arbitrary")),
        name="mla_attn",
    )(qm, km, vt)


def _out_kernel(ma_ref, ob_ref, x_ref, mod_ref, gb_ref, wo_ref, g2_ref, wr_ref, br_ref,
                x1_ref, h2_ref, rt_ref, rtm_ref):
    ob = ob_ref[0]
    mb = (ob * lax.rsqrt(jnp.mean(ob * ob, axis=-1, keepdims=True) + EPS) * gb_ref[...]).astype(BF16)
    y = _dot(ma_ref[0], wo_ref[0:SWA_WIDTH, :]) + _dot(mb, wo_ref[SWA_WIDTH:, :])
    x1 = x_ref[0] + mod_ref[0, 2:3, :] * y
    x1_ref[0] = x1
    h = x1 * lax.rsqrt(jnp.mean(x1 * x1, axis=-1, keepdims=True) + EPS) * g2_ref[...]
    h = h * (1.0 + mod_ref[0, 4:5, :]) + mod_ref[0, 3:4, :]
    h_hi, h_lo = _split_bf16(h)
    h2_ref[0] = h_hi

    tm = h.shape[0]
    l_hi = _dot_nt(wr_ref[...], h_hi)
    l_lo = _dot_nt(wr_ref[0:ROUTE_ROWS, :], h_lo)
    logit = l_hi[0:ROUTE_ROWS] + l_hi[ROUTE_ROWS:] + l_lo + br_ref[...]
    row8 = lax.broadcasted_iota(jnp.int32, (8, tm), 0)
    gl = jnp.where(row8 < N_GROUPS, logit[0:8], NEG)
    gmax = jnp.max(gl, axis=0, keepdims=True)
    gsum = jnp.sum(jnp.exp(gl - gmax), axis=0, keepdims=True)
    g_idx = jnp.min(jnp.where(gl == gmax, row8, 8), axis=0, keepdims=True)
    g_w = 1.0 / gsum
    ec = logit[8:16]
    for g in range(1, N_GROUPS):
        ec = jnp.where(g_idx == g, logit[8 + 8 * g:16 + 8 * g], ec)
    m1 = jnp.max(ec, axis=0, keepdims=True)
    i1 = jnp.min(jnp.where(ec == m1, row8, 8), axis=0, keepdims=True)
    ec2 = jnp.where(row8 == i1, NEG, ec)
    m2 = jnp.max(ec2, axis=0, keepdims=True)
    i2 = jnp.min(jnp.where(ec2 == m2, row8, 8), axis=0, keepdims=True)
    t = jnp.exp(m2 - m1)
    w1 = g_w / (1.0 + t)
    w2 = g_w * t / (1.0 + t)
    e1 = (g_idx * EXPERTS_PER_GROUP + i1).astype(F32)
    e2 = (g_idx * EXPERTS_PER_GROUP + i2).astype(F32)
    route = jnp.concatenate([e1, e2, w1, w2, jnp.zeros((4, tm), F32)], axis=0)
    rt_ref[...] = route
    rtm_ref[...] = jnp.concatenate([route, jnp.zeros((LANES - 8, tm), F32)], axis=0).T


def _outproj(mixa, ob, x, mod3, g_mla, w_out, g2, wr, br, tm):
    b, s, _ = x.shape
    nt = s // tm
    full = lambda shp: pl.BlockSpec(shp, lambda i, j: (0,) * len(shp))
    tok = lambda w: pl.BlockSpec((1, tm, w), lambda i, j: (i, j, 0))
    return pl.pallas_call(
        _out_kernel,
        grid=(b, nt),
        in_specs=[tok(SWA_WIDTH), tok(MLA_WIDTH), tok(D_MODEL),
                  pl.BlockSpec((1, N_MOD, D_MODEL), lambda i, j: (i, 0, 0)),
                  full((1, MLA_WIDTH)), full((D_MODEL, D_MODEL)), full((1, D_MODEL)),
                  full((2 * ROUTE_ROWS, D_MODEL)), full((ROUTE_ROWS, 1))],
        out_specs=[tok(D_MODEL), tok(D_MODEL),
                   pl.BlockSpec((8, tm), lambda i, j: (0, i * nt + j)),
                   pl.BlockSpec((tm, LANES), lambda i, j: (i * nt + j, 0))],
        out_shape=[jax.ShapeDtypeStruct((b, s, D_MODEL), F32),
                   jax.ShapeDtypeStruct((b, s, D_MODEL), BF16),
                   jax.ShapeDtypeStruct((8, b * s), F32),
                   jax.ShapeDtypeStruct((b * s, LANES), F32)],
        compiler_params=_params(("arbitrary", "arbitrary")),
        name="outproj_router",
    )(mixa, ob, x, mod3, g_mla, w_out, g2, wr, br)


def _moe_kernel(h_ref, rt_ref, wg_ref, wu_ref, wd_ref, x1_ref, mod_ref, gf_ref, o_ref, acc_ref):
    e = pl.program_id(1)

    @pl.when(e == 0)
    def _():
        acc_ref[...] = jnp.zeros_like(acc_ref)

    h = h_ref[...]
    g = _dot(h, wg_ref[0])
    u = _dot(h, wu_ref[0])
    ef = e.astype(F32)
    cw = (jnp.where(rt_ref[:, 0:1] == ef, rt_ref[:, 2:3], 0.0)
          + jnp.where(rt_ref[:, 1:2] == ef, rt_ref[:, 3:4], 0.0))
    hid = (g * (1.0 / (1.0 + jnp.exp(-g))) * u * cw).astype(BF16)
    acc_ref[...] += _dot(hid, wd_ref[0])

    @pl.when(e == N_EXPERTS - 1)
    def _():
        y = x1_ref[...] + mod_ref[0, 5:6, :] * acc_ref[...]
        o_ref[...] = y * lax.rsqrt(jnp.mean(y * y, axis=-1, keepdims=True) + EPS) * gf_ref[...]


def _moe_dense(h2, rtm, wg, wu, wd, x1, mod3, gf, seq, tm):
    t = h2.shape[0]
    per_b = seq // tm
    tok = lambda w: pl.BlockSpec((tm, w), lambda i, e: (i, 0))
    return pl.pallas_call(
        _moe_kernel,
        grid=(t // tm, N_EXPERTS),
        in_specs=[tok(D_MODEL), tok(LANES),
                  pl.BlockSpec((1, D_MODEL, D_EXPERT), lambda i, e: (e, 0, 0)),
                  pl.BlockSpec((1, D_MODEL, D_EXPERT), lambda i, e: (e, 0, 0)),
                  pl.BlockSpec((1, D_EXPERT, D_MODEL), lambda i, e: (e, 0, 0)),
                  tok(D_MODEL),
                  pl.BlockSpec((1, N_MOD, D_MODEL), lambda i, e: (i // per_b, 0, 0)),
                  pl.BlockSpec((1, D_MODEL), lambda i, e: (0, 0))],
        out_specs=tok(D_MODEL),
        out_shape=jax.ShapeDtypeStruct((t, D_MODEL), F32),
        scratch_shapes=[pltpu.VMEM((tm, D_MODEL), F32)],
        compiler_params=_params(("arbitrary", "arbitrary")),
        name="moe_final",
    )(h2, rtm, wg, wu, wd, x1, mod3, gf)


def _rope_tables(seq, dim):
    inv = 1.0 / (ROPE_THETA ** (jnp.arange(0, dim, 2, dtype=F32) / dim))
    ang = jnp.arange(seq, dtype=F32)[:, None] * inv[None, :]
    return jnp.cos(ang), jnp.sin(ang)


def _tables(seq):
    ca, sa = _rope_tables(seq, SWA_HEAD_DIM)
    cb, sb = _rope_tables(seq, MLA_ROPE)
    z32 = jnp.zeros_like(sa)
    z16 = jnp.zeros_like(sb)
    z64 = jnp.zeros((seq, 64), F32)
    one64 = jnp.ones((seq, 64), F32)
    cat = lambda parts: jnp.concatenate(parts, axis=1)
    scale = float((MLA_NOPE + MLA_ROPE) ** -0.5)
    swa = [cat([ca] * 4), cat([z32, sa] * 2), cat([-sa, z32] * 2)]
    mq = [cat([one64, cb, cb, z32]) * scale, cat([z64, z16, sb, z32]) * scale, cat([z64, -sb, z16, z32]) * scale]
    mk = [cat([z64, cb, cb, z32]), cat([z64, z16, sb, z32]), cat([z64, -sb, z16, z32])]
    return jnp.stack(swa + mq + mk, axis=0)


def _layout_weights(w_in, w_uq, w_ukv, g_q):
    qa, ka, va, cq, ckv, kr = jnp.split(w_in, np.cumsum([512, 128, 128, 192, 128, 32])[:-1].tolist(), axis=1)
    zc = lambda n: jnp.zeros((D_MODEL, n), F32)
    w_in_l = jnp.concatenate([qa * (SWA_HEAD_DIM ** -0.5), ka, va, ckv, cq, zc(64), zc(64), kr, zc(32)], axis=1)
    uq = w_uq.reshape(MLA_Q_RANK, MLA_HEADS, MLA_NOPE + MLA_ROPE)
    uq = jnp.pad(uq, ((0, MLA_Q_RANK_PAD - MLA_Q_RANK), (0, 0), (0, LANES - MLA_NOPE - MLA_ROPE)))
    ukv = w_ukv.reshape(MLA_KV_RANK, MLA_HEADS, MLA_NOPE + MLA_V)
    wk = jnp.pad(ukv[:, :, :MLA_NOPE], ((0, 0), (0, 0), (0, LANES - MLA_NOPE)))
    wvt = ukv[:, :, MLA_NOPE:].reshape(MLA_KV_RANK, MLA_WIDTH).T
    gq = jnp.pad(g_q, (0, MLA_Q_RANK_PAD - MLA_Q_RANK))[None, :]
    return (w_in_l.astype(BF16), uq.reshape(MLA_Q_RANK_PAD, MLA_HEADS * LANES).astype(BF16),
            wk.reshape(MLA_KV_RANK, MLA_HEADS * LANES).astype(BF16), wvt.astype(BF16), gq)


def _layout_router(w_rg, b_rg, w_re, b_re):
    w = jnp.zeros((ROUTE_ROWS, D_MODEL), F32)
    w = w.at[0:N_GROUPS].set(w_rg.T).at[8:8 + N_EXPERTS].set(w_re.T)
    bias = jnp.zeros((ROUTE_ROWS,), F32).at[0:N_GROUPS].set(b_rg).at[8:8 + N_EXPERTS].set(b_re)
    hi = w.astype(BF16)
    lo = (w - hi.astype(F32)).astype(BF16)
    return jnp.concatenate([hi, lo], axis=0), bias[:, None]


def kernel(x, c, w_ada, b_ada, g_norm1, w_in, g_q_lora, w_uq, g_kv_lora, w_ukv, sink, g_out_swa, g_out_mla, w_out, g_norm2, w_router_group, b_router_group, w_router_expert, b_router_expert, w_exp_gate, w_exp_up, w_exp_down, g_final):
    b, s, d = x.shape
    assert d == D_MODEL and s % 512 == 0 and b <= 8
    assert w_ada.shape[0] == 1, "single layer"

    c8 = jnp.pad(c, ((0, 8 - b), (0, 0)))
    mod3 = _ada(c8, w_ada[0], b_ada[0][None, :]).reshape(8, N_MOD, D_MODEL)

    w_in_l, w_uq_l, w_k_l, w_vt_l, gq = _layout_weights(w_in[0], w_uq[0], w_ukv[0], g_q_lora[0])
    qa, ka, va, qm, km, vt = _proj(x, mod3, g_norm1[0][None, :], _tables(s), w_in_l, gq,
                                   g_kv_lora[0][None, :], w_uq_l, w_k_l, w_vt_l, tm=256)

    mixa = _swa(sink[0][None, :], qa, ka, va, g_out_swa[0][None, :])
    ob = _mla(qm, km, vt, tq=256, tk=512)

    wr, br = _layout_router(w_router_group[0], b_router_group[0], w_router_expert[0], b_router_expert[0])
    x1, h2, _, rtm = _outproj(mixa, ob, x, mod3, g_out_mla[0][None, :], w_out[0].astype(BF16),
                              g_norm2[0][None, :], wr, br, tm=256)

    out = _moe_dense(h2.reshape(b * s, d), rtm, w_exp_gate[0].astype(BF16), w_exp_up[0].astype(BF16),
                     w_exp_down[0].astype(BF16), x1.reshape(b * s, d), mod3, g_final[None, :], seq=s, tm=512)
    return out.reshape(b, s, d)
```

```python
import functools

import jax
import jax.numpy as jnp
import numpy as np
from jax import lax
from jax.experimental import pallas as pl
from jax.experimental.pallas import tpu as pltpu

F32 = jnp.float32
BF16 = jnp.bfloat16

D_MODEL = 1024
EPS = 1e-6
ROPE_THETA = 10000.0
NEG = -1e30
LANES = 128

BLOCK = 128
SWA_HEADS = 8
SWA_KV_HEADS = 2
SWA_HEAD_DIM = 64
SWA_WIDTH = SWA_HEADS * SWA_HEAD_DIM

MLA_HEADS = 8
MLA_NOPE = 64
MLA_ROPE = 32
MLA_V = 64
MLA_Q_RANK = 192
MLA_Q_RANK_PAD = 256
MLA_KV_RANK = 128
MLA_WIDTH = MLA_HEADS * MLA_V

N_GROUPS = 4
EXPERTS_PER_GROUP = 8
N_EXPERTS = 32
D_EXPERT = 256
N_MOD = 6

COL_QA = 0
COL_KA = 512
COL_VA = 640
COL_CKV = 768
COL_CQ = 896
COL_KR = 1152
IN_COLS = 1280

ROUTE_ROWS = 48

VMEM_LIMIT = 56 * 1024 * 1024

NT_DIMS = (((1,), (1,)), ((), ()))


def _dot(a, b):
    return jnp.dot(a, b, preferred_element_type=F32)


def _dot_nt(a, b):
    return lax.dot_general(a, b, NT_DIMS, preferred_element_type=F32)


def _split_bf16(a):
    hi = a.astype(BF16)
    lo = (a - hi.astype(F32)).astype(BF16)
    return hi, lo


def _params(sem):
    return pltpu.CompilerParams(dimension_semantics=sem, vmem_limit_bytes=VMEM_LIMIT)


def _ada_kernel(c_ref, w_ref, b_ref, o_ref):
    c = c_ref[...]
    a = c * (1.0 / (1.0 + jnp.exp(-c)))
    a_hi, a_lo = _split_bf16(a)
    w_hi, w_lo = _split_bf16(w_ref[...])
    o_ref[...] = _dot(a_hi, w_hi) + _dot(a_lo, w_hi) + _dot(a_hi, w_lo) + b_ref[...]


def _ada(c8, w_ada, b_ada):
    n = w_ada.shape[1]
    tn = 1024
    return pl.pallas_call(
        _ada_kernel,
        grid=(n // tn,),
        in_specs=[pl.BlockSpec((8, D_MODEL), lambda j: (0, 0)),
                  pl.BlockSpec((D_MODEL, tn), lambda j: (0, j)),
                  pl.BlockSpec((1, tn), lambda j: (0, j))],
        out_specs=pl.BlockSpec((8, tn), lambda j: (0, j)),
        out_shape=jax.ShapeDtypeStruct((8, n), F32),
        compiler_params=_params(("arbitrary",)),
        name="ada_mod",
    )(c8, w_ada, b_ada)


def _rope_slab(x, c, s1, s2, sh):
    return x * c + pltpu.roll(x, sh, 1) * s1 + pltpu.roll(x, LANES - sh, 1) * s2


def _proj_kernel(x_ref, mod_ref, g1_ref, tab_ref, win_ref, gq_ref, gkv_ref, wuq_ref, wk_ref, wvt_ref,
                 qa_ref, ka_ref, va_ref, qm_ref, km_ref, vt_ref):
    x = x_ref[0]
    ms = jnp.mean(x * x, axis=-1, keepdims=True)
    h = x * lax.rsqrt(ms + EPS) * g1_ref[...]
    h = h * (1.0 + mod_ref[0, 1:2, :]) + mod_ref[0, 0:1, :]
    proj = _dot(h.astype(BF16), win_ref[...])

    ca, sa1, sa2 = tab_ref[0], tab_ref[1], tab_ref[2]
    for s in range(SWA_WIDTH // LANES):
        q = proj[:, COL_QA + s * LANES:COL_QA + (s + 1) * LANES]
        qa_ref[0, :, s * LANES:(s + 1) * LANES] = _rope_slab(q, ca, sa1, sa2, 32).astype(BF16)
    ka_ref[0] = _rope_slab(proj[:, COL_KA:COL_KA + LANES], ca, sa1, sa2, 32).astype(BF16)
    va_ref[0] = proj[:, COL_VA:COL_VA + LANES].astype(BF16)

    ckv = proj[:, COL_CKV:COL_CKV + MLA_KV_RANK]
    ckvn = (ckv * lax.rsqrt(jnp.mean(ckv * ckv, axis=-1, keepdims=True) + EPS) * gkv_ref[...]).astype(BF16)
    cq = proj[:, COL_CQ:COL_CQ + MLA_Q_RANK_PAD]
    cq_ms = jnp.sum(cq * cq, axis=-1, keepdims=True) * (1.0 / MLA_Q_RANK)
    cqn = (cq * lax.rsqrt(cq_ms + EPS) * gq_ref[...]).astype(BF16)

    kr = _rope_slab(proj[:, COL_KR:COL_KR + LANES], tab_ref[6], tab_ref[7], tab_ref[8], 16)
    qm = _dot(cqn, wuq_ref[...])
    kn = _dot(ckvn, wk_ref[...])
    cq_t, sq1, sq2 = tab_ref[3], tab_ref[4], tab_ref[5]
    for hd in range(MLA_HEADS):
        sl = slice(hd * LANES, (hd + 1) * LANES)
        qm_ref[0, hd] = _rope_slab(qm[:, sl], cq_t, sq1, sq2, 16).astype(BF16)
        km_ref[0, hd] = (kn[:, sl] + kr).astype(BF16)
    vt_ref[0] = _dot_nt(wvt_ref[...], ckvn).astype(BF16)


def _proj(x, mod3, g1, tabs, w_in, gq, gkv, w_uq, w_k, w_vt, tm):
    b, s, _ = x.shape
    full = lambda shp: pl.BlockSpec(shp, lambda i, j: (0,) * len(shp))
    return pl.pallas_call(
        _proj_kernel,
        grid=(s // tm, b),
        in_specs=[pl.BlockSpec((1, tm, D_MODEL), lambda i, j: (j, i, 0)),
                  pl.BlockSpec((1, N_MOD, D_MODEL), lambda i, j: (j, 0, 0)),
                  full((1, D_MODEL)),
                  pl.BlockSpec((9, tm, LANES), lambda i, j: (0, i, 0)),
                  full((D_MODEL, IN_COLS)),
                  full((1, MLA_Q_RANK_PAD)),
                  full((1, MLA_KV_RANK)),
                  full((MLA_Q_RANK_PAD, MLA_HEADS * LANES)),
                  full((MLA_KV_RANK, MLA_HEADS * LANES)),
                  full((MLA_WIDTH, MLA_KV_RANK))],
        out_specs=[pl.BlockSpec((1, tm, SWA_WIDTH), lambda i, j: (j, i, 0)),
                   pl.BlockSpec((1, tm, LANES), lambda i, j: (j, i, 0)),
                   pl.BlockSpec((1, tm, LANES), lambda i, j: (j, i, 0)),
                   pl.BlockSpec((1, MLA_HEADS, tm, LANES), lambda i, j: (j, 0, i, 0)),
                   pl.BlockSpec((1, MLA_HEADS, tm, LANES), lambda i, j: (j, 0, i, 0)),
                   pl.BlockSpec((1, MLA_WIDTH, tm), lambda i, j: (j, 0, i))],
        out_shape=[jax.ShapeDtypeStruct((b, s, SWA_WIDTH), BF16),
                   jax.ShapeDtypeStruct((b, s, LANES), BF16),
                   jax.ShapeDtypeStruct((b, s, LANES), BF16),
                   jax.ShapeDtypeStruct((b, MLA_HEADS, s, LANES), BF16),
                   jax.ShapeDtypeStruct((b, MLA_HEADS, s, LANES), BF16),
                   jax.ShapeDtypeStruct((b, MLA_WIDTH, s), BF16)],
        compiler_params=_params(("arbitrary", "arbitrary")),
        name="norm1_proj",
    )(x, mod3, g1, tabs, w_in, gq, gkv, w_uq, w_k, w_vt)


def _swa_kernel(sink_ref, q_ref, kp_ref, kc_ref, kn_ref, vp_ref, vc_ref, vn_ref, g_ref, o_ref, *, nb):
    n = pl.program_id(1)
    k = jnp.concatenate([kp_ref[0], kc_ref[0], kn_ref[0]], axis=0)
    v = jnp.concatenate([vp_ref[0], vc_ref[0], vn_ref[0]], axis=0)
    k_sw = pltpu.roll(k, SWA_HEAD_DIM, 1)
    v_sw = pltpu.roll(v, SWA_HEAD_DIM, 1)
    lane = lax.broadcasted_iota(jnp.int32, k.shape, 1)
    lo = lane < SWA_HEAD_DIM
    zero = jnp.zeros_like(k)
    k_var = ((jnp.where(lo, k, zero), jnp.where(lo, zero, k_sw)),
             (jnp.where(lo, k_sw, zero), jnp.where(lo, zero, k)))
    v_var = ((jnp.where(lo, v, zero), jnp.where(lo, zero, v_sw)),
             (jnp.where(lo, v_sw, zero), jnp.where(lo, zero, v)))

    r = lax.broadcasted_iota(jnp.int32, (BLOCK, 3 * BLOCK), 0)
    j = lax.broadcasted_iota(jnp.int32, (BLOCK, 3 * BLOCK), 1)
    mask = (j >= r) & (j <= r + 2 * BLOCK)
    mask = mask & ((j >= BLOCK) | (n > 0)) & ((j < 2 * BLOCK) | (n < nb - 1))

    slabs = []
    ssq = jnp.zeros((BLOCK, 1), F32)
    group = SWA_HEADS // SWA_KV_HEADS
    for s in range(SWA_WIDTH // LANES):
        q = q_ref[0, :, s * LANES:(s + 1) * LANES]
        o_slab = jnp.zeros((BLOCK, LANES), F32)
        for half in range(2):
            head = 2 * s + half
            kv = head // group
            sc = _dot_nt(q, k_var[kv][half])
            sc = jnp.where(mask, sc, NEG)
            snk = sink_ref[0, head]
            m = jnp.maximum(jnp.max(sc, axis=-1, keepdims=True), snk)
            p = jnp.exp(sc - m)
            denom = jnp.sum(p, axis=-1, keepdims=True) + jnp.exp(snk - m)
            o_slab = o_slab + _dot(p.astype(BF16), v_var[kv][half]) * (1.0 / denom)
        slabs.append(o_slab)
        ssq = ssq + jnp.sum(o_slab * o_slab, axis=-1, keepdims=True)
    scale = lax.rsqrt(ssq * (1.0 / SWA_WIDTH) + EPS)
    for s, o_slab in enumerate(slabs):
        sl = slice(s * LANES, (s + 1) * LANES)
        o_ref[0, :, sl] = (o_slab * scale * g_ref[:, sl]).astype(BF16)


def _swa(sink, qa, ka, va, g_out):
    b, s, _ = qa.shape
    nb = s // BLOCK
    kv_spec = lambda f: pl.BlockSpec((1, BLOCK, LANES), f)
    prev = lambda i, n: (i, jnp.maximum(n - 1, 0), 0)
    cur = lambda i, n: (i, n, 0)
    nxt = lambda i, n: (i, jnp.minimum(n + 1, nb - 1), 0)
    return pl.pallas_call(
        functools.partial(_swa_kernel, nb=nb),
        grid=(b, nb),
        in_specs=[pl.BlockSpec(memory_space=pltpu.SMEM),
                  pl.BlockSpec((1, BLOCK, SWA_WIDTH), cur),
                  kv_spec(prev), kv_spec(cur), kv_spec(nxt),
                  kv_spec(prev), kv_spec(cur), kv_spec(nxt),
                  pl.BlockSpec((1, SWA_WIDTH), lambda i, n: (0, 0))],
        out_specs=pl.BlockSpec((1, BLOCK, SWA_WIDTH), cur),
        out_shape=jax.ShapeDtypeStruct((b, s, SWA_WIDTH), BF16),
        compiler_params=_params(("arbitrary", "arbitrary")),
        name="swa_attn",
    )(sink, qa, ka, ka, ka, va, va, va, g_out)


def _mla_kernel(q_ref, k_ref, vt_ref, o_ref, *, tk):
    tq = q_ref.shape[2]
    n_chunks = k_ref.shape[2] // tk
    ones = jnp.ones((16, tk), BF16)
    qs = (q_ref[0, 0], q_ref[0, 1])

    def scores(c):
        off = pl.multiple_of(c * tk, tk)
        return tuple(_dot_nt(k_ref[0, hd, pl.ds(off, tk), :], qs[hd]) for hd in range(2))

    def absorb(c, st, m, acc):
        off = pl.multiple_of(c * tk, tk)
        m_out, acc_out = [], []
        for hd in range(2):
            m_new = jnp.maximum(m[hd], jnp.max(st[hd], axis=0, keepdims=True))
            alpha = jnp.exp2(m[hd] - m_new)
            p = jnp.exp2(st[hd] - m_new).astype(BF16)
            v = vt_ref[0, hd * MLA_V:(hd + 1) * MLA_V, pl.ds(off, tk)]
            vext = jnp.concatenate([v, ones], axis=0)
            m_out.append(m_new)
            acc_out.append(acc[hd] * alpha + _dot(vext, p))
        return tuple(m_out), tuple(acc_out)

    def body(c, carry):
        st, m, acc = carry
        st_next = scores(c + 1)
        m, acc = absorb(c, st, m, acc)
        return st_next, m, acc

    m0 = (jnp.full((1, tq), NEG, F32),) * 2
    acc0 = (jnp.zeros((MLA_V + 16, tq), F32),) * 2
    st, m, acc = lax.fori_loop(0, n_chunks - 1, body, (scores(0), m0, acc0), unroll=True)
    _, acc = absorb(n_chunks - 1, st, m, acc)
    outs = [a[:MLA_V] * (1.0 / a[MLA_V:MLA_V + 1]) for a in acc]
    o_ref[0] = jnp.concatenate(outs, axis=0).T


def _mla(qm, km, vt, tq, tk):
    b, h, s, _ = qm.shape
    return pl.pallas_call(
        functools.partial(_mla_kernel, tk=tk),
        grid=(b, h // 2, s // tq),
        in_specs=[pl.BlockSpec((1, 2, tq, LANES), lambda i, p, t: (i, p, t, 0)),
                  pl.BlockSpec((1, 2, s, LANES), lambda i, p, t: (i, p, 0, 0)),
                  pl.BlockSpec((1, 2 * MLA_V, s), lambda i, p, t: (i, p, 0))],
        out_specs=pl.BlockSpec((1, tq, LANES), lambda i, p, t: (i, t, p)),
        out_shape=jax.ShapeDtypeStruct((b, s, MLA_WIDTH), F32),
        compiler_params=_params(("arbitrary", "arbitrary", "arbitrary")),
        name="mla_attn",
    )(qm, km, vt)


def _out_kernel(ma_ref, ob_ref, x_ref, mod_ref, gb_ref, wo_ref, g2_ref, wr_ref, br_ref,
                x1_ref, h2_ref, rt_ref, rtm_ref):
    ob = ob_ref[0]
    mb = (ob * lax.rsqrt(jnp.mean(ob * ob, axis=-1, keepdims=True) + EPS) * gb_ref[...]).astype(BF16)
    y = _dot(ma_ref[0], wo_ref[0:SWA_WIDTH, :]) + _dot(mb, wo_ref[SWA_WIDTH:, :])
    x1 = x_ref[0] + mod_ref[0, 2:3, :] * y
    x1_ref[0] = x1
    h = x1 * lax.rsqrt(jnp.mean(x1 * x1, axis=-1, keepdims=True) + EPS) * g2_ref[...]
    h = h * (1.0 + mod_ref[0, 4:5, :]) + mod_ref[0, 3:4, :]
    h_hi, h_lo = _split_bf16(h)
    h2_ref[0] = h_hi

    tm = h.shape[0]
    l_hi = _dot_nt(wr_ref[...], h_hi)
    l_lo = _dot_nt(wr_ref[0:ROUTE_ROWS, :], h_lo)
    logit = l_hi[0:ROUTE_ROWS] + l_hi[ROUTE_ROWS:] + l_lo + br_ref[...]
    row8 = lax.broadcasted_iota(jnp.int32, (8, tm), 0)
    gl = jnp.where(row8 < N_GROUPS, logit[0:8], NEG)
    gmax = jnp.max(gl, axis=0, keepdims=True)
    gsum = jnp.sum(jnp.exp(gl - gmax), axis=0, keepdims=True)
    g_idx = jnp.min(jnp.where(gl == gmax, row8, 8), axis=0, keepdims=True)
    g_w = 1.0 / gsum
    ec = logit[8:16]
    for g in range(1, N_GROUPS):
        ec = jnp.where(g_idx == g, logit[8 + 8 * g:16 + 8 * g], ec)
    m1 = jnp.max(ec, axis=0, keepdims=True)
    i1 = jnp.min(jnp.where(ec == m1, row8, 8), axis=0, keepdims=True)
    ec2 = jnp.where(row8 == i1, NEG, ec)
    m2 = jnp.max(ec2, axis=0, keepdims=True)
    i2 = jnp.min(jnp.where(ec2 == m2, row8, 8), axis=0, keepdims=True)
    t = jnp.exp(m2 - m1)
    w1 = g_w / (1.0 + t)
    w2 = g_w * t / (1.0 + t)
    e1 = (g_idx * EXPERTS_PER_GROUP + i1).astype(F32)
    e2 = (g_idx * EXPERTS_PER_GROUP + i2).astype(F32)
    route = jnp.concatenate([e1, e2, w1, w2, jnp.zeros((4, tm), F32)], axis=0)
    rt_ref[...] = route
    rtm_ref[...] = jnp.concatenate([route, jnp.zeros((LANES - 8, tm), F32)], axis=0).T


def _outproj(mixa, ob, x, mod3, g_mla, w_out, g2, wr, br, tm):
    b, s, _ = x.shape
    nt = s // tm
    full = lambda shp: pl.BlockSpec(shp, lambda i, j: (0,) * len(shp))
    tok = lambda w: pl.BlockSpec((1, tm, w), lambda i, j: (i, j, 0))
    return pl.pallas_call(
        _out_kernel,
        grid=(b, nt),
        in_specs=[tok(SWA_WIDTH), tok(MLA_WIDTH), tok(D_MODEL),
                  pl.BlockSpec((1, N_MOD, D_MODEL), lambda i, j: (i, 0, 0)),
                  full((1, MLA_WIDTH)), full((D_MODEL, D_MODEL)), full((1, D_MODEL)),
                  full((2 * ROUTE_ROWS, D_MODEL)), full((ROUTE_ROWS, 1))],
        out_specs=[tok(D_MODEL), tok(D_MODEL),
                   pl.BlockSpec((8, tm), lambda i, j: (0, i * nt + j)),
                   pl.BlockSpec((tm, LANES), lambda i, j: (i * nt + j, 0))],
        out_shape=[jax.ShapeDtypeStruct((b, s, D_MODEL), F32),
                   jax.ShapeDtypeStruct((b, s, D_MODEL), BF16),
                   jax.ShapeDtypeStruct((8, b * s), F32),
                   jax.ShapeDtypeStruct((b * s, LANES), F32)],
        compiler_params=_params(("arbitrary", "arbitrary")),
        name="outproj_router",
    )(mixa, ob, x, mod3, g_mla, w_out, g2, wr, br)


def _moe_kernel(h_ref, rt_ref, wg_ref, wu_ref, wd_ref, x1_ref, mod_ref, gf_ref, o_ref, acc_ref):
    e = pl.program_id(1)

    @pl.when(e == 0)
    def _():
        acc_ref[...] = jnp.zeros_like(acc_ref)

    h = h_ref[...]
    g = _dot(h, wg_ref[0])
    u = _dot(h, wu_ref[0])
    ef = e.astype(F32)
    cw = (jnp.where(rt_ref[:, 0:1] == ef, rt_ref[:, 2:3], 0.0)
          + jnp.where(rt_ref[:, 1:2] == ef, rt_ref[:, 3:4], 0.0))
    hid = (g * (1.0 / (1.0 + jnp.exp(-g))) * u * cw).astype(BF16)
    acc_ref[...] += _dot(hid, wd_ref[0])

    @pl.when(e == N_EXPERTS - 1)
    def _():
        y = x1_ref[...] + mod_ref[0, 5:6, :] * acc_ref[...]
        o_ref[...] = y * lax.rsqrt(jnp.mean(y * y, axis=-1, keepdims=True) + EPS) * gf_ref[...]


def _moe_dense(h2, rtm, wg, wu, wd, x1, mod3, gf, seq, tm):
    t = h2.shape[0]
    per_b = seq // tm
    tok = lambda w: pl.BlockSpec((tm, w), lambda i, e: (i, 0))
    return pl.pallas_call(
        _moe_kernel,
        grid=(t // tm, N_EXPERTS),
        in_specs=[tok(D_MODEL), tok(LANES),
                  pl.BlockSpec((1, D_MODEL, D_EXPERT), lambda i, e: (e, 0, 0)),
                  pl.BlockSpec((1, D_MODEL, D_EXPERT), lambda i, e: (e, 0, 0)),
                  pl.BlockSpec((1, D_EXPERT, D_MODEL), lambda i, e: (e, 0, 0)),
                  tok(D_MODEL),
                  pl.BlockSpec((1, N_MOD, D_MODEL), lambda i, e: (i // per_b, 0, 0)),
                  pl.BlockSpec((1, D_MODEL), lambda i, e: (0, 0))],
        out_specs=tok(D_MODEL),
        out_shape=jax.ShapeDtypeStruct((t, D_MODEL), F32),
        scratch_shapes=[pltpu.VMEM((tm, D_MODEL), F32)],
        compiler_params=_params(("arbitrary", "arbitrary")),
        name="moe_final",
    )(h2, rtm, wg, wu, wd, x1, mod3, gf)


def _rope_tables(seq, dim):
    inv = 1.0 / (ROPE_THETA ** (jnp.arange(0, dim, 2, dtype=F32) / dim))
    ang = jnp.arange(seq, dtype=F32)[:, None] * inv[None, :]
    return jnp.cos(ang), jnp.sin(ang)


def _tables(seq):
    ca, sa = _rope_tables(seq, SWA_HEAD_DIM)
    cb, sb = _rope_tables(seq, MLA_ROPE)
    z32 = jnp.zeros_like(sa)
    z16 = jnp.zeros_like(sb)
    z64 = jnp.zeros((seq, 64), F32)
    one64 = jnp.ones((seq, 64), F32)
    cat = lambda parts: jnp.concatenate(parts, axis=1)
    scale = float((MLA_NOPE + MLA_ROPE) ** -0.5 * np.log2(np.e))
    swa = [cat([ca] * 4), cat([z32, sa] * 2), cat([-sa, z32] * 2)]
    mq = [cat([one64, cb, cb, z32]) * scale, cat([z64, z16, sb, z32]) * scale, cat([z64, -sb, z16, z32]) * scale]
    mk = [cat([z64, cb, cb, z32]), cat([z64, z16, sb, z32]), cat([z64, -sb, z16, z32])]
    return jnp.stack(swa + mq + mk, axis=0)


def _layout_weights(w_in, w_uq, w_ukv, g_q):
    qa, ka, va, cq, ckv, kr = jnp.split(w_in, np.cumsum([512, 128, 128, 192, 128, 32])[:-1].tolist(), axis=1)
    zc = lambda n: jnp.zeros((D_MODEL, n), F32)
    w_in_l = jnp.concatenate([qa * (SWA_HEAD_DIM ** -0.5), ka, va, ckv, cq, zc(64), zc(64), kr, zc(32)], axis=1)
    uq = w_uq.reshape(MLA_Q_RANK, MLA_HEADS, MLA_NOPE + MLA_ROPE)
    uq = jnp.pad(uq, ((0, MLA_Q_RANK_PAD - MLA_Q_RANK), (0, 0), (0, LANES - MLA_NOPE - MLA_ROPE)))
    ukv = w_ukv.reshape(MLA_KV_RANK, MLA_HEADS, MLA_NOPE + MLA_V)
    wk = jnp.pad(ukv[:, :, :MLA_NOPE], ((0, 0), (0, 0), (0, LANES - MLA_NOPE)))
    wvt = ukv[:, :, MLA_NOPE:].reshape(MLA_KV_RANK, MLA_WIDTH).T
    gq = jnp.pad(g_q, (0, MLA_Q_RANK_PAD - MLA_Q_RANK))[None, :]
    return (w_in_l.astype(BF16), uq.reshape(MLA_Q_RANK_PAD, MLA_HEADS * LANES).astype(BF16),
            wk.reshape(MLA_KV_RANK, MLA_HEADS * LANES).astype(BF16), wvt.astype(BF16), gq)


def _layout_router(w_rg, b_rg, w_re, b_re):
    w = jnp.zeros((ROUTE_ROWS, D_MODEL), F32)
    w = w.at[0:N_GROUPS].set(w_rg.T).at[8:8 + N_EXPERTS].set(w_re.T)
    bias = jnp.zeros((ROUTE_ROWS,), F32).at[0:N_GROUPS].set(b_rg).at[8:8 + N_EXPERTS].set(b_re)
    hi = w.astype(BF16)
    lo = (w - hi.astype(F32)).astype(BF16)
    return jnp.concatenate([hi, lo], axis=0), bias[:, None]


def kernel(x, c, w_ada, b_ada, g_norm1, w_in, g_q_lora, w_uq, g_kv_lora, w_ukv, sink, g_out_swa, g_out_mla, w_out, g_norm2, w_router_group, b_router_group, w_router_expert, b_router_expert, w_exp_gate, w_exp_up, w_exp_down, g_final):
    b, s, d = x.shape
    assert d == D_MODEL and s % 512 == 0 and b <= 8
    assert w_ada.shape[0] == 1, "single layer"

    c8 = jnp.pad(c, ((0, 8 - b), (0, 0)))
    mod3 = _ada(c8, w_ada[0], b_ada[0][None, :]).reshape(8, N_MOD, D_MODEL)

    w_in_l, w_uq_l, w_k_l, w_vt_l, gq = _layout_weights(w_in[0], w_uq[0], w_ukv[0], g_q_lora[0])
    qa, ka, va, qm, km, vt = _proj(x, mod3, g_norm1[0][None, :], _tables(s), w_in_l, gq,
                                   g_kv_lora[0][None, :], w_uq_l, w_k_l, w_vt_l, tm=256)

    mixa = _swa(sink[0][None, :], qa, ka, va, g_out_swa[0][None, :])
    ob = _mla(qm, km, vt, tq=256, tk=256)

    wr, br = _layout_router(w_router_group[0], b_router_group[0], w_router_expert[0], b_router_expert[0])
    x1, h2, _, rtm = _outproj(mixa, ob, x, mod3, g_out_mla[0][None, :], w_out[0].astype(BF16),
                              g_norm2[0][None, :], wr, br, tm=256)

    out = _moe_dense(h2.reshape(b * s, d), rtm, w_exp_gate[0].astype(BF16), w_exp_up[0].astype(BF16),
                     w_exp_down[0].astype(BF16), x1.reshape(b * s, d), mod3, g_final[None, :], seq=s, tm=512)
    return out.reshape(b, s, d)
```

```python
import functools

import jax
import jax.numpy as jnp
import numpy as np
from jax import lax
from jax.experimental import pallas as pl
from jax.experimental.pallas import tpu as pltpu

F32 = jnp.float32
BF16 = jnp.bfloat16

D_MODEL = 1024
EPS = 1e-6
ROPE_THETA = 10000.0
NEG = -1e30
LANES = 128

BLOCK = 128
SWA_HEADS = 8
SWA_KV_HEADS = 2
SWA_HEAD_DIM = 64
SWA_WIDTH = SWA_HEADS * SWA_HEAD_DIM

MLA_HEADS = 8
MLA_NOPE = 64
MLA_ROPE = 32
MLA_V = 64
MLA_Q_RANK = 192
MLA_Q_RANK_PAD = 256
MLA_KV_RANK = 128
MLA_WIDTH = MLA_HEADS * MLA_V

N_GROUPS = 4
EXPERTS_PER_GROUP = 8
N_EXPERTS = 32
D_EXPERT = 256
N_MOD = 6

COL_QA = 0
COL_KA = 512
COL_VA = 640
COL_CKV = 768
COL_CQ = 896
COL_KR = 1152
IN_COLS = 1280

EXPERT_TILE = 256
ROUTE_ROWS = 48

VMEM_LIMIT = 56 * 1024 * 1024

NT_DIMS = (((1,), (1,)), ((), ()))


def _dot(a, b):
    return jnp.dot(a, b, preferred_element_type=F32)


def _dot_nt(a, b):
    return lax.dot_general(a, b, NT_DIMS, preferred_element_type=F32)


def _split_bf16(a):
    hi = a.astype(BF16)
    lo = (a - hi.astype(F32)).astype(BF16)
    return hi, lo


def _params(sem):
    return pltpu.CompilerParams(dimension_semantics=sem, vmem_limit_bytes=VMEM_LIMIT)


def _ada_kernel(c_ref, w_ref, b_ref, o_ref):
    c = c_ref[...]
    a = c * (1.0 / (1.0 + jnp.exp(-c)))
    a_hi, a_lo = _split_bf16(a)
    w_hi, w_lo = _split_bf16(w_ref[...])
    o_ref[...] = _dot(a_hi, w_hi) + _dot(a_lo, w_hi) + _dot(a_hi, w_lo) + b_ref[...]


def _ada(c8, w_ada, b_ada):
    n = w_ada.shape[1]
    tn = 1024
    return pl.pallas_call(
        _ada_kernel,
        grid=(n // tn,),
        in_specs=[pl.BlockSpec((8, D_MODEL), lambda j: (0, 0)),
                  pl.BlockSpec((D_MODEL, tn), lambda j: (0, j)),
                  pl.BlockSpec((1, tn), lambda j: (0, j))],
        out_specs=pl.BlockSpec((8, tn), lambda j: (0, j)),
        out_shape=jax.ShapeDtypeStruct((8, n), F32),
        compiler_params=_params(("arbitrary",)),
        name="ada_mod",
    )(c8, w_ada, b_ada)


def _rope_slab(x, c, s1, s2, sh):
    return x * c + pltpu.roll(x, sh, 1) * s1 + pltpu.roll(x, LANES - sh, 1) * s2


def _proj_kernel(x_ref, mod_ref, g1_ref, tab_ref, win_ref, gq_ref, gkv_ref, wuq_ref, wk_ref, wvt_ref,
                 qa_ref, ka_ref, va_ref, qm_ref, km_ref, vt_ref):
    x = x_ref[0]
    ms = jnp.mean(x * x, axis=-1, keepdims=True)
    h = x * lax.rsqrt(ms + EPS) * g1_ref[...]
    h = h * (1.0 + mod_ref[0, 1:2, :]) + mod_ref[0, 0:1, :]
    proj = _dot(h.astype(BF16), win_ref[...])

    ca, sa1, sa2 = tab_ref[0], tab_ref[1], tab_ref[2]
    for s in range(SWA_WIDTH // LANES):
        q = proj[:, COL_QA + s * LANES:COL_QA + (s + 1) * LANES]
        qa_ref[0, :, s * LANES:(s + 1) * LANES] = _rope_slab(q, ca, sa1, sa2, 32).astype(BF16)
    ka_ref[0] = _rope_slab(proj[:, COL_KA:COL_KA + LANES], ca, sa1, sa2, 32).astype(BF16)
    va_ref[0] = proj[:, COL_VA:COL_VA + LANES].astype(BF16)

    ckv = proj[:, COL_CKV:COL_CKV + MLA_KV_RANK]
    ckvn = (ckv * lax.rsqrt(jnp.mean(ckv * ckv, axis=-1, keepdims=True) + EPS) * gkv_ref[...]).astype(BF16)
    cq = proj[:, COL_CQ:COL_CQ + MLA_Q_RANK_PAD]
    cq_ms = jnp.sum(cq * cq, axis=-1, keepdims=True) * (1.0 / MLA_Q_RANK)
    cqn = (cq * lax.rsqrt(cq_ms + EPS) * gq_ref[...]).astype(BF16)

    kr = _rope_slab(proj[:, COL_KR:COL_KR + LANES], tab_ref[6], tab_ref[7], tab_ref[8], 16)
    qm = _dot(cqn, wuq_ref[...])
    kn = _dot(ckvn, wk_ref[...])
    cq_t, sq1, sq2 = tab_ref[3], tab_ref[4], tab_ref[5]
    for hd in range(MLA_HEADS):
        sl = slice(hd * LANES, (hd + 1) * LANES)
        qm_ref[0, hd] = _rope_slab(qm[:, sl], cq_t, sq1, sq2, 16).astype(BF16)
        km_ref[0, hd] = (kn[:, sl] + kr).astype(BF16)
    vt_ref[0] = _dot_nt(wvt_ref[...], ckvn).astype(BF16)


def _proj(x, mod3, g1, tabs, w_in, gq, gkv, w_uq, w_k, w_vt, tm):
    b, s, _ = x.shape
    full = lambda shp: pl.BlockSpec(shp, lambda i, j: (0,) * len(shp))
    return pl.pallas_call(
        _proj_kernel,
        grid=(s // tm, b),
        in_specs=[pl.BlockSpec((1, tm, D_MODEL), lambda i, j: (j, i, 0)),
                  pl.BlockSpec((1, N_MOD, D_MODEL), lambda i, j: (j, 0, 0)),
                  full((1, D_MODEL)),
                  pl.BlockSpec((9, tm, LANES), lambda i, j: (0, i, 0)),
                  full((D_MODEL, IN_COLS)),
                  full((1, MLA_Q_RANK_PAD)),
                  full((1, MLA_KV_RANK)),
                  full((MLA_Q_RANK_PAD, MLA_HEADS * LANES)),
                  full((MLA_KV_RANK, MLA_HEADS * LANES)),
                  full((MLA_WIDTH, MLA_KV_RANK))],
        out_specs=[pl.BlockSpec((1, tm, SWA_WIDTH), lambda i, j: (j, i, 0)),
                   pl.BlockSpec((1, tm, LANES), lambda i, j: (j, i, 0)),
                   pl.BlockSpec((1, tm, LANES), lambda i, j: (j, i, 0)),
                   pl.BlockSpec((1, MLA_HEADS, tm, LANES), lambda i, j: (j, 0, i, 0)),
                   pl.BlockSpec((1, MLA_HEADS, tm, LANES), lambda i, j: (j, 0, i, 0)),
                   pl.BlockSpec((1, MLA_WIDTH, tm), lambda i, j: (j, 0, i))],
        out_shape=[jax.ShapeDtypeStruct((b, s, SWA_WIDTH), BF16),
                   jax.ShapeDtypeStruct((b, s, LANES), BF16),
                   jax.ShapeDtypeStruct((b, s, LANES), BF16),
                   jax.ShapeDtypeStruct((b, MLA_HEADS, s, LANES), BF16),
                   jax.ShapeDtypeStruct((b, MLA_HEADS, s, LANES), BF16),
                   jax.ShapeDtypeStruct((b, MLA_WIDTH, s), BF16)],
        compiler_params=_params(("arbitrary", "arbitrary")),
        name="norm1_proj",
    )(x, mod3, g1, tabs, w_in, gq, gkv, w_uq, w_k, w_vt)


def _swa_kernel(sink_ref, q_ref, kp_ref, kc_ref, kn_ref, vp_ref, vc_ref, vn_ref, g_ref, o_ref, *, nb):
    n = pl.program_id(1)
    k = jnp.concatenate([kp_ref[0], kc_ref[0], kn_ref[0]], axis=0)
    v = jnp.concatenate([vp_ref[0], vc_ref[0], vn_ref[0]], axis=0)
    k_sw = pltpu.roll(k, SWA_HEAD_DIM, 1)
    v_sw = pltpu.roll(v, SWA_HEAD_DIM, 1)
    lane = lax.broadcasted_iota(jnp.int32, k.shape, 1)
    lo = lane < SWA_HEAD_DIM
    zero = jnp.zeros_like(k)
    k_var = ((jnp.where(lo, k, zero), jnp.where(lo, zero, k_sw)),
             (jnp.where(lo, k_sw, zero), jnp.where(lo, zero, k)))
    v_var = ((jnp.where(lo, v, zero), jnp.where(lo, zero, v_sw)),
             (jnp.where(lo, v_sw, zero), jnp.where(lo, zero, v)))

    r = lax.broadcasted_iota(jnp.int32, (BLOCK, 3 * BLOCK), 0)
    j = lax.broadcasted_iota(jnp.int32, (BLOCK, 3 * BLOCK), 1)
    mask = (j >= r) & (j <= r + 2 * BLOCK)
    mask = mask & ((j >= BLOCK) | (n > 0)) & ((j < 2 * BLOCK) | (n < nb - 1))

    slabs = []
    ssq = jnp.zeros((BLOCK, 1), F32)
    group = SWA_HEADS // SWA_KV_HEADS
    for s in range(SWA_WIDTH // LANES):
        q = q_ref[0, :, s * LANES:(s + 1) * LANES]
        o_slab = jnp.zeros((BLOCK, LANES), F32)
        for half in range(2):
            head = 2 * s + half
            kv = head // group
            sc = _dot_nt(q, k_var[kv][half])
            sc = jnp.where(mask, sc, NEG)
            snk = sink_ref[0, head]
            m = jnp.maximum(jnp.max(sc, axis=-1, keepdims=True), snk)
            p = jnp.exp(sc - m)
            denom = jnp.sum(p, axis=-1, keepdims=True) + jnp.exp(snk - m)
            o_slab = o_slab + _dot(p.astype(BF16), v_var[kv][half]) * (1.0 / denom)
        slabs.append(o_slab)
        ssq = ssq + jnp.sum(o_slab * o_slab, axis=-1, keepdims=True)
    scale = lax.rsqrt(ssq * (1.0 / SWA_WIDTH) + EPS)
    for s, o_slab in enumerate(slabs):
        sl = slice(s * LANES, (s + 1) * LANES)
        o_ref[0, :, sl] = (o_slab * scale * g_ref[:, sl]).astype(BF16)


def _swa(sink, qa, ka, va, g_out):
    b, s, _ = qa.shape
    nb = s // BLOCK
    kv_spec = lambda f: pl.BlockSpec((1, BLOCK, LANES), f)
    prev = lambda i, n: (i, jnp.maximum(n - 1, 0), 0)
    cur = lambda i, n: (i, n, 0)
    nxt = lambda i, n: (i, jnp.minimum(n + 1, nb - 1), 0)
    return pl.pallas_call(
        functools.partial(_swa_kernel, nb=nb),
        grid=(b, nb),
        in_specs=[pl.BlockSpec(memory_space=pltpu.SMEM),
                  pl.BlockSpec((1, BLOCK, SWA_WIDTH), cur),
                  kv_spec(prev), kv_spec(cur), kv_spec(nxt),
                  kv_spec(prev), kv_spec(cur), kv_spec(nxt),
                  pl.BlockSpec((1, SWA_WIDTH), lambda i, n: (0, 0))],
        out_specs=pl.BlockSpec((1, BLOCK, SWA_WIDTH), cur),
        out_shape=jax.ShapeDtypeStruct((b, s, SWA_WIDTH), BF16),
        compiler_params=_params(("arbitrary", "arbitrary")),
        name="swa_attn",
    )(sink, qa, ka, ka, ka, va, va, va, g_out)


def _mla_kernel(q_ref, k_ref, vt_ref, o_ref, *, tk):
    tq = q_ref.shape[2]
    n_chunks = k_ref.shape[2] // tk
    ones = jnp.ones((16, tk), BF16)
    qs = (q_ref[0, 0], q_ref[0, 1])

    def scores(c):
        off = pl.multiple_of(c * tk, tk)
        return tuple(_dot_nt(k_ref[0, hd, pl.ds(off, tk), :], qs[hd]) for hd in range(2))

    def absorb(c, st, m, acc):
        off = pl.multiple_of(c * tk, tk)
        m_out, acc_out = [], []
        for hd in range(2):
            m_new = jnp.maximum(m[hd], jnp.max(st[hd], axis=0, keepdims=True))
            alpha = jnp.exp2(m[hd] - m_new)
            p = jnp.exp2(st[hd] - m_new).astype(BF16)
            v = vt_ref[0, hd * MLA_V:(hd + 1) * MLA_V, pl.ds(off, tk)]
            vext = jnp.concatenate([v, ones], axis=0)
            m_out.append(m_new)
            acc_out.append(acc[hd] * alpha + _dot(vext, p))
        return tuple(m_out), tuple(acc_out)

    def body(c, carry):
        st, m, acc = carry
        st_next = scores(c + 1)
        m, acc = absorb(c, st, m, acc)
        return st_next, m, acc

    m0 = (jnp.full((1, tq), NEG, F32),) * 2
    acc0 = (jnp.zeros((MLA_V + 16, tq), F32),) * 2
    st, m, acc = lax.fori_loop(0, n_chunks - 1, body, (scores(0), m0, acc0), unroll=True)
    _, acc = absorb(n_chunks - 1, st, m, acc)
    outs = [a[:MLA_V] * (1.0 / a[MLA_V:MLA_V + 1]) for a in acc]
    o_ref[0] = jnp.concatenate(outs, axis=0).T


def _mla(qm, km, vt, tq, tk):
    b, h, s, _ = qm.shape
    return pl.pallas_call(
        functools.partial(_mla_kernel, tk=tk),
        grid=(b, h // 2, s // tq),
        in_specs=[pl.BlockSpec((1, 2, tq, LANES), lambda i, p, t: (i, p, t, 0)),
                  pl.BlockSpec((1, 2, s, LANES), lambda i, p, t: (i, p, 0, 0)),
                  pl.BlockSpec((1, 2 * MLA_V, s), lambda i, p, t: (i, p, 0))],
        out_specs=pl.BlockSpec((1, tq, LANES), lambda i, p, t: (i, t, p)),
        out_shape=jax.ShapeDtypeStruct((b, s, MLA_WIDTH), F32),
        compiler_params=_params(("arbitrary", "arbitrary", "arbitrary")),
        name="mla_attn",
    )(qm, km, vt)


def _out_kernel(ma_ref, ob_ref, x_ref, mod_ref, gb_ref, wo_ref, g2_ref, wr_ref, br_ref, tri_ref,
                x1_ref, h2_ref, rt_ref, rtm_ref, cnt_ref, carry_ref):
    first = (pl.program_id(0) == 0) & (pl.program_id(1) == 0)

    @pl.when(first)
    def _():
        carry_ref[...] = jnp.zeros_like(carry_ref)

    ob = ob_ref[0]
    mb = (ob * lax.rsqrt(jnp.mean(ob * ob, axis=-1, keepdims=True) + EPS) * gb_ref[...]).astype(BF16)
    y = _dot(ma_ref[0], wo_ref[0:SWA_WIDTH, :]) + _dot(mb, wo_ref[SWA_WIDTH:, :])
    x1 = x_ref[0] + mod_ref[0, 2:3, :] * y
    x1_ref[0] = x1
    h = x1 * lax.rsqrt(jnp.mean(x1 * x1, axis=-1, keepdims=True) + EPS) * g2_ref[...]
    h = h * (1.0 + mod_ref[0, 4:5, :]) + mod_ref[0, 3:4, :]
    h2_ref[0] = h
    h_hi, h_lo = _split_bf16(h)

    tm = h.shape[0]
    l_hi = _dot_nt(wr_ref[...], h_hi)
    l_lo = _dot_nt(wr_ref[0:ROUTE_ROWS, :], h_lo)
    logit = l_hi[0:ROUTE_ROWS] + l_hi[ROUTE_ROWS:] + l_lo + br_ref[...]
    row8 = lax.broadcasted_iota(jnp.int32, (8, tm), 0)
    gl = jnp.where(row8 < N_GROUPS, logit[0:8], NEG)
    gmax = jnp.max(gl, axis=0, keepdims=True)
    gsum = jnp.sum(jnp.exp(gl - gmax), axis=0, keepdims=True)
    g_idx = jnp.min(jnp.where(gl == gmax, row8, 8), axis=0, keepdims=True)
    g_w = 1.0 / gsum
    ec = logit[8:16]
    for g in range(1, N_GROUPS):
        ec = jnp.where(g_idx == g, logit[8 + 8 * g:16 + 8 * g], ec)
    m1 = jnp.max(ec, axis=0, keepdims=True)
    i1 = jnp.min(jnp.where(ec == m1, row8, 8), axis=0, keepdims=True)
    ec2 = jnp.where(row8 == i1, NEG, ec)
    m2 = jnp.max(ec2, axis=0, keepdims=True)
    i2 = jnp.min(jnp.where(ec2 == m2, row8, 8), axis=0, keepdims=True)
    t = jnp.exp(m2 - m1)
    w1 = g_w / (1.0 + t)
    w2 = g_w * t / (1.0 + t)
    e1 = g_idx * EXPERTS_PER_GROUP + i1
    e2 = g_idx * EXPERTS_PER_GROUP + i2

    rows = lax.broadcasted_iota(jnp.int32, (N_EXPERTS, tm), 0)
    oh1 = (rows == e1).astype(F32)
    oh2 = (rows == e2).astype(F32)
    both = oh1 + oh2
    incl = _dot(both.astype(BF16), tri_ref[...])
    before = carry_ref[:, 0:1] + incl - both
    r1 = jnp.sum(oh1 * before, axis=0, keepdims=True)
    r2 = jnp.sum(oh2 * before, axis=0, keepdims=True)
    total = carry_ref[:, 0:1] + incl[:, tm - 1:tm]
    carry_ref[...] = jnp.broadcast_to(total, carry_ref.shape)
    cnt_ref[...] = jnp.broadcast_to(total, cnt_ref.shape)

    route = jnp.concatenate([e1.astype(F32), e2.astype(F32), w1, w2, r1, r2, jnp.zeros((2, tm), F32)], axis=0)
    rt_ref[...] = route
    rtm_ref[...] = jnp.concatenate([route, jnp.zeros((LANES - 8, tm), F32)], axis=0).T


def _outproj(mixa, ob, x, mod3, g_mla, w_out, g2, wr, br, tm):
    b, s, _ = x.shape
    nt = s // tm
    tri = jnp.triu(jnp.ones((tm, tm), F32)).astype(BF16)
    full = lambda shp: pl.BlockSpec(shp, lambda i, j: (0,) * len(shp))
    tok = lambda w: pl.BlockSpec((1, tm, w), lambda i, j: (i, j, 0))
    return pl.pallas_call(
        _out_kernel,
        grid=(b, nt),
        in_specs=[tok(SWA_WIDTH), tok(MLA_WIDTH), tok(D_MODEL),
                  pl.BlockSpec((1, N_MOD, D_MODEL), lambda i, j: (i, 0, 0)),
                  full((1, MLA_WIDTH)), full((D_MODEL, D_MODEL)), full((1, D_MODEL)),
                  full((2 * ROUTE_ROWS, D_MODEL)), full((ROUTE_ROWS, 1)), full((tm, tm))],
        out_specs=[tok(D_MODEL), tok(D_MODEL),
                   pl.BlockSpec((8, tm), lambda i, j: (0, i * nt + j)),
                   pl.BlockSpec((tm, LANES), lambda i, j: (i * nt + j, 0)),
                   full((N_EXPERTS, LANES))],
        out_shape=[jax.ShapeDtypeStruct((b, s, D_MODEL), F32),
                   jax.ShapeDtypeStruct((b, s, D_MODEL), F32),
                   jax.ShapeDtypeStruct((8, b * s), F32),
                   jax.ShapeDtypeStruct((b * s, LANES), F32),
                   jax.ShapeDtypeStruct((N_EXPERTS, LANES), F32)],
        scratch_shapes=[pltpu.VMEM((N_EXPERTS, LANES), F32)],
        compiler_params=_params(("arbitrary", "arbitrary")),
        name="outproj_router",
    )(mixa, ob, x, mod3, g_mla, w_out, g2, wr, br, tri)


def _row_copy(src_ref, src_row, dst_ref, dst_row, sem):
    return pltpu.make_async_copy(src_ref.at[pl.ds(src_row, 1)], dst_ref.at[pl.ds(dst_row, 1)], sem)


def _dispatch_kernel(slot_ref, zst_ref, h_ref, xs_ref, zbuf, sem, zsem, *, tmd):
    i = pl.program_id(0)

    @pl.when(i == 0)
    def _():
        zbuf[...] = jnp.zeros_like(zbuf)
        for phase in ("start", "wait"):
            for e in range(N_EXPERTS):
                z = zst_ref[e]

                @pl.when(z >= 0)
                def _():
                    zrow = pl.multiple_of(z, EXPERT_TILE)
                    cp = pltpu.make_async_copy(zbuf, xs_ref.at[pl.ds(zrow, EXPERT_TILE)], zsem)
                    cp.start() if phase == "start" else cp.wait()

    base = i * tmd

    def issue(r, carry):
        for k in range(2):
            _row_copy(h_ref, base + r, xs_ref, slot_ref[0, 0, 2 * r + k], sem).start()
        return carry

    def drain(r, carry):
        _row_copy(h_ref, 0, xs_ref, 0, sem).wait()
        return carry

    lax.fori_loop(0, tmd, issue, 0, unroll=4)
    lax.fori_loop(0, 2 * tmd, drain, 0, unroll=8)


def _dispatch(slots3, zstart, h2, n_slots, tmd):
    t = h2.shape[0]
    return pl.pallas_call(
        functools.partial(_dispatch_kernel, tmd=tmd),
        grid=(t // tmd,),
        in_specs=[pl.BlockSpec((1, 1, 2 * tmd), lambda i: (i, 0, 0), memory_space=pltpu.SMEM),
                  pl.BlockSpec(memory_space=pltpu.SMEM),
                  pl.BlockSpec(memory_space=pl.ANY)],
        out_specs=pl.BlockSpec(memory_space=pl.ANY),
        out_shape=jax.ShapeDtypeStruct((n_slots, D_MODEL), F32),
        scratch_shapes=[pltpu.VMEM((EXPERT_TILE, D_MODEL), F32),
                        pltpu.SemaphoreType.DMA(()), pltpu.SemaphoreType.DMA(())],
        compiler_params=_params(("arbitrary",)),
        name="moe_dispatch",
    )(slots3, zstart, h2)


def _expert_kernel(te_ref, nt_ref, xs_ref, wg_ref, wu_ref, wd_ref, y_ref, wgu_buf, wd_buf):
    i = pl.program_id(0)

    @pl.when(i < nt_ref[0])
    def _():
        changed = (i == 0) | (te_ref[i] != te_ref[jnp.maximum(i - 1, 0)])

        @pl.when(changed)
        def _():
            wgu_buf[:, 0:D_EXPERT] = wg_ref[0].astype(BF16)
            wgu_buf[:, D_EXPERT:] = wu_ref[0].astype(BF16)
            wd_buf[...] = wd_ref[0].astype(BF16)

        gu = _dot(xs_ref[...].astype(BF16), wgu_buf[...])
        g = gu[:, 0:D_EXPERT]
        hid = (g * (1.0 / (1.0 + jnp.exp(-g))) * gu[:, D_EXPERT:]).astype(BF16)
        y_ref[...] = _dot(hid, wd_buf[...])


def _experts(tile_expert, n_tiles, xs, wg, wu, wd):
    max_tiles = xs.shape[0] // EXPERT_TILE
    row = lambda i, te, nt: (jnp.minimum(i, nt[0] - 1), 0)
    wsel = lambda i, te, nt: (te[i], 0, 0)
    return pl.pallas_call(
        _expert_kernel,
        grid_spec=pltpu.PrefetchScalarGridSpec(
            num_scalar_prefetch=2,
            grid=(max_tiles,),
            in_specs=[pl.BlockSpec((EXPERT_TILE, D_MODEL), row),
                      pl.BlockSpec((1, D_MODEL, D_EXPERT), wsel),
                      pl.BlockSpec((1, D_MODEL, D_EXPERT), wsel),
                      pl.BlockSpec((1, D_EXPERT, D_MODEL), wsel)],
            out_specs=pl.BlockSpec((EXPERT_TILE, D_MODEL), row),
            scratch_shapes=[pltpu.VMEM((D_MODEL, 2 * D_EXPERT), BF16),
                            pltpu.VMEM((D_EXPERT, D_MODEL), BF16)]),
        out_shape=jax.ShapeDtypeStruct(xs.shape, F32),
        compiler_params=_params(("arbitrary",)),
        name="moe_experts",
    )(tile_expert, n_tiles, xs, wg, wu, wd)


def _combine_kernel(slot_ref, y_ref, rt_ref, x1_ref, mod_ref, gf_ref, o_ref, ybuf, sem, *, tmc):
    def issue(r, carry):
        for k in range(2):
            _row_copy(y_ref, slot_ref[0, 0, 2 * r + k], ybuf.at[k], r, sem).start()
        return carry

    def drain(r, carry):
        _row_copy(y_ref, 0, ybuf.at[0], 0, sem).wait()
        return carry

    lax.fori_loop(0, tmc, issue, 0, unroll=4)
    lax.fori_loop(0, 2 * tmc, drain, 0, unroll=8)
    moe = rt_ref[:, 2:3] * ybuf[0] + rt_ref[:, 3:4] * ybuf[1]
    v = x1_ref[...] + mod_ref[0, 5:6, :] * moe
    o_ref[...] = v * lax.rsqrt(jnp.mean(v * v, axis=-1, keepdims=True) + EPS) * gf_ref[...]


def _combine(slots3, y, rtm, x1, mod3, gf, seq, tmc):
    t = x1.shape[0]
    per_b = seq // tmc
    tok = lambda w: pl.BlockSpec((tmc, w), lambda i: (i, 0))
    return pl.pallas_call(
        functools.partial(_combine_kernel, tmc=tmc),
        grid=(t // tmc,),
        in_specs=[pl.BlockSpec((1, 1, 2 * tmc), lambda i: (i, 0, 0), memory_space=pltpu.SMEM),
                  pl.BlockSpec(memory_space=pl.ANY),
                  tok(LANES), tok(D_MODEL),
                  pl.BlockSpec((1, N_MOD, D_MODEL), lambda i: (i // per_b, 0, 0)),
                  pl.BlockSpec((1, D_MODEL), lambda i: (0, 0))],
        out_specs=tok(D_MODEL),
        out_shape=jax.ShapeDtypeStruct((t, D_MODEL), F32),
        scratch_shapes=[pltpu.VMEM((2, tmc, D_MODEL), F32), pltpu.SemaphoreType.DMA(())],
        compiler_params=_params(("arbitrary",)),
        name="moe_combine_final",
    )(slots3, y, rtm, x1, mod3, gf)


def _plan_slots(route_t, counts, n_tokens):
    e = route_t[0:2].astype(jnp.int32)
    rank = route_t[4:6].astype(jnp.int32)
    cnt = counts[:, 0].astype(jnp.int32)
    tiles = (cnt + EXPERT_TILE - 1) // EXPERT_TILE
    ids = jnp.arange(N_EXPERTS, dtype=jnp.int32)
    tile_end = jnp.sum(jnp.where(ids[:, None] <= ids[None, :], tiles[:, None], 0), axis=0)
    base = (tile_end - tiles) * EXPERT_TILE
    slots = (base[e] + rank).T
    max_tiles = (2 * n_tokens + N_EXPERTS * (EXPERT_TILE - 1)) // EXPERT_TILE
    tile_ids = jnp.arange(max_tiles, dtype=jnp.int32)
    tile_expert = jnp.minimum(jnp.sum((tile_ids[:, None] >= tile_end[None, :]).astype(jnp.int32), axis=1),
                              N_EXPERTS - 1)
    zstart = jnp.where(tiles > 0, (tile_end - 1) * EXPERT_TILE, -1).astype(jnp.int32)
    return slots, tile_expert, tile_end[-1:].astype(jnp.int32), zstart, max_tiles * EXPERT_TILE


def _rope_tables(seq, dim):
    inv = 1.0 / (ROPE_THETA ** (jnp.arange(0, dim, 2, dtype=F32) / dim))
    ang = jnp.arange(seq, dtype=F32)[:, None] * inv[None, :]
    return jnp.cos(ang), jnp.sin(ang)


def _tables(seq):
    ca, sa = _rope_tables(seq, SWA_HEAD_DIM)
    cb, sb = _rope_tables(seq, MLA_ROPE)
    z32 = jnp.zeros_like(sa)
    z16 = jnp.zeros_like(sb)
    z64 = jnp.zeros((seq, 64), F32)
    one64 = jnp.ones((seq, 64), F32)
    cat = lambda parts: jnp.concatenate(parts, axis=1)
    scale = float((MLA_NOPE + MLA_ROPE) ** -0.5 * np.log2(np.e))
    swa = [cat([ca] * 4), cat([z32, sa] * 2), cat([-sa, z32] * 2)]
    mq = [cat([one64, cb, cb, z32]) * scale, cat([z64, z16, sb, z32]) * scale, cat([z64, -sb, z16, z32]) * scale]
    mk = [cat([z64, cb, cb, z32]), cat([z64, z16, sb, z32]), cat([z64, -sb, z16, z32])]
    return jnp.stack(swa + mq + mk, axis=0)


def _layout_weights(w_in, w_uq, w_ukv, g_q):
    qa, ka, va, cq, ckv, kr = jnp.split(w_in, np.cumsum([512, 128, 128, 192, 128, 32])[:-1].tolist(), axis=1)
    zc = lambda n: jnp.zeros((D_MODEL, n), F32)
    w_in_l = jnp.concatenate([qa * (SWA_HEAD_DIM ** -0.5), ka, va, ckv, cq, zc(64), zc(64), kr, zc(32)], axis=1)
    uq = w_uq.reshape(MLA_Q_RANK, MLA_HEADS, MLA_NOPE + MLA_ROPE)
    uq = jnp.pad(uq, ((0, MLA_Q_RANK_PAD - MLA_Q_RANK), (0, 0), (0, LANES - MLA_NOPE - MLA_ROPE)))
    ukv = w_ukv.reshape(MLA_KV_RANK, MLA_HEADS, MLA_NOPE + MLA_V)
    wk = jnp.pad(ukv[:, :, :MLA_NOPE], ((0, 0), (0, 0), (0, LANES - MLA_NOPE)))
    wvt = ukv[:, :, MLA_NOPE:].reshape(MLA_KV_RANK, MLA_WIDTH).T
    gq = jnp.pad(g_q, (0, MLA_Q_RANK_PAD - MLA_Q_RANK))[None, :]
    return (w_in_l.astype(BF16), uq.reshape(MLA_Q_RANK_PAD, MLA_HEADS * LANES).astype(BF16),
            wk.reshape(MLA_KV_RANK, MLA_HEADS * LANES).astype(BF16), wvt.astype(BF16), gq)


def _layout_router(w_rg, b_rg, w_re, b_re):
    w = jnp.zeros((ROUTE_ROWS, D_MODEL), F32)
    w = w.at[0:N_GROUPS].set(w_rg.T).at[8:8 + N_EXPERTS].set(w_re.T)
    bias = jnp.zeros((ROUTE_ROWS,), F32).at[0:N_GROUPS].set(b_rg).at[8:8 + N_EXPERTS].set(b_re)
    hi = w.astype(BF16)
    lo = (w - hi.astype(F32)).astype(BF16)
    return jnp.concatenate([hi, lo], axis=0), bias[:, None]


def kernel(x, c, w_ada, b_ada, g_norm1, w_in, g_q_lora, w_uq, g_kv_lora, w_ukv, sink, g_out_swa, g_out_mla, w_out, g_norm2, w_router_group, b_router_group, w_router_expert, b_router_expert, w_exp_gate, w_exp_up, w_exp_down, g_final):
    b, s, d = x.shape
    assert d == D_MODEL and s % 512 == 0 and b <= 8
    assert w_ada.shape[0] == 1, "single layer"

    c8 = jnp.pad(c, ((0, 8 - b), (0, 0)))
    mod3 = _ada(c8, w_ada[0], b_ada[0][None, :]).reshape(8, N_MOD, D_MODEL)

    w_in_l, w_uq_l, w_k_l, w_vt_l, gq = _layout_weights(w_in[0], w_uq[0], w_ukv[0], g_q_lora[0])
    qa, ka, va, qm, km, vt = _proj(x, mod3, g_norm1[0][None, :], _tables(s), w_in_l, gq,
                                   g_kv_lora[0][None, :], w_uq_l, w_k_l, w_vt_l, tm=256)

    mixa = _swa(sink[0][None, :], qa, ka, va, g_out_swa[0][None, :])
    ob = _mla(qm, km, vt, tq=256, tk=256)

    wr, br = _layout_router(w_router_group[0], b_router_group[0], w_router_expert[0], b_router_expert[0])
    x1, h2, route_t, rtm, counts = _outproj(mixa, ob, x, mod3, g_out_mla[0][None, :], w_out[0].astype(BF16),
                                            g_norm2[0][None, :], wr, br, tm=256)

    t = b * s
    slots, tile_expert, n_tiles, zstart, n_slots = _plan_slots(route_t, counts, t)
    tmd, tmc = 512, 256
    xs = _dispatch(slots.reshape(t // tmd, 1, 2 * tmd), zstart, h2.reshape(t, d), n_slots, tmd)
    y = _experts(tile_expert, n_tiles, xs, w_exp_gate[0], w_exp_up[0], w_exp_down[0])
    out = _combine(slots.reshape(t // tmc, 1, 2 * tmc), y, rtm, x1.reshape(t, d), mod3, g_final[None, :], s, tmc)
    return out.reshape(b, s, d)
```

```python
import functools

import jax
import jax.numpy as jnp
import numpy as np
from jax import lax
from jax.experimental import pallas as pl
from jax.experimental.pallas import tpu as pltpu

F32 = jnp.float32
BF16 = jnp.bfloat16

D_MODEL = 1024
EPS = 1e-6
ROPE_THETA = 10000.0
NEG = -1e30
LANES = 128

BLOCK = 128
SWA_HEADS = 8
SWA_KV_HEADS = 2
SWA_HEAD_DIM = 64
SWA_WIDTH = SWA_HEADS * SWA_HEAD_DIM

MLA_HEADS = 8
MLA_NOPE = 64
MLA_ROPE = 32
MLA_V = 64
MLA_Q_RANK = 192
MLA_Q_RANK_PAD = 256
MLA_KV_RANK = 128
MLA_WIDTH = MLA_HEADS * MLA_V

N_GROUPS = 4
EXPERTS_PER_GROUP = 8
N_EXPERTS = 32
D_EXPERT = 256
N_MOD = 6

COL_QA = 0
COL_KA = 512
COL_VA = 640
COL_CKV = 768
COL_CQ = 896
COL_KR = 1152
IN_COLS = 1280

EXPERT_TILE = 256
ROUTE_ROWS = 48

VMEM_LIMIT = 56 * 1024 * 1024

NT_DIMS = (((1,), (1,)), ((), ()))


def _dot(a, b):
    return jnp.dot(a, b, preferred_element_type=F32)


def _dot_nt(a, b):
    return lax.dot_general(a, b, NT_DIMS, preferred_element_type=F32)


def _split_bf16(a):
    hi = a.astype(BF16)
    lo = (a - hi.astype(F32)).astype(BF16)
    return hi, lo


def _params(sem):
    return pltpu.CompilerParams(dimension_semantics=sem, vmem_limit_bytes=VMEM_LIMIT)


def _ada_kernel(c_ref, w_ref, b_ref, o_ref):
    c = c_ref[...]
    a = c * (1.0 / (1.0 + jnp.exp(-c)))
    a_hi, a_lo = _split_bf16(a)
    w_hi, w_lo = _split_bf16(w_ref[...])
    o_ref[...] = _dot(a_hi, w_hi) + _dot(a_lo, w_hi) + _dot(a_hi, w_lo) + b_ref[...]


def _ada(c8, w_ada, b_ada):
    n = w_ada.shape[1]
    tn = 1024
    return pl.pallas_call(
        _ada_kernel,
        grid=(n // tn,),
        in_specs=[pl.BlockSpec((8, D_MODEL), lambda j: (0, 0)),
                  pl.BlockSpec((D_MODEL, tn), lambda j: (0, j)),
                  pl.BlockSpec((1, tn), lambda j: (0, j))],
        out_specs=pl.BlockSpec((8, tn), lambda j: (0, j)),
        out_shape=jax.ShapeDtypeStruct((8, n), F32),
        compiler_params=_params(("arbitrary",)),
        name="ada_mod",
    )(c8, w_ada, b_ada)


def _rope_slab(x, c, s1, s2, sh):
    return x * c + pltpu.roll(x, sh, 1) * s1 + pltpu.roll(x, LANES - sh, 1) * s2


def _proj_kernel(x_ref, mod_ref, g1_ref, tab_ref, win_ref, gq_ref, gkv_ref, wuq_ref, wk_ref, wvt_ref,
                 qa_ref, ka_ref, va_ref, qm_ref, km_ref, vt_ref):
    x = x_ref[0]
    ms = jnp.mean(x * x, axis=-1, keepdims=True)
    h = x * lax.rsqrt(ms + EPS) * g1_ref[...]
    h = h * (1.0 + mod_ref[0, 1:2, :]) + mod_ref[0, 0:1, :]
    proj = _dot(h.astype(BF16), win_ref[...])

    ca, sa1, sa2 = tab_ref[0], tab_ref[1], tab_ref[2]
    for s in range(SWA_WIDTH // LANES):
        q = proj[:, COL_QA + s * LANES:COL_QA + (s + 1) * LANES]
        qa_ref[0, :, s * LANES:(s + 1) * LANES] = _rope_slab(q, ca, sa1, sa2, 32).astype(BF16)
    ka_ref[0] = _rope_slab(proj[:, COL_KA:COL_KA + LANES], ca, sa1, sa2, 32).astype(BF16)
    va_ref[0] = proj[:, COL_VA:COL_VA + LANES].astype(BF16)

    ckv = proj[:, COL_CKV:COL_CKV + MLA_KV_RANK]
    ckvn = (ckv * lax.rsqrt(jnp.mean(ckv * ckv, axis=-1, keepdims=True) + EPS) * gkv_ref[...]).astype(BF16)
    cq = proj[:, COL_CQ:COL_CQ + MLA_Q_RANK_PAD]
    cq_ms = jnp.sum(cq * cq, axis=-1, keepdims=True) * (1.0 / MLA_Q_RANK)
    cqn = (cq * lax.rsqrt(cq_ms + EPS) * gq_ref[...]).astype(BF16)

    kr = _rope_slab(proj[:, COL_KR:COL_KR + LANES], tab_ref[6], tab_ref[7], tab_ref[8], 16)
    qm = _dot(cqn, wuq_ref[...])
    kn = _dot(ckvn, wk_ref[...])
    cq_t, sq1, sq2 = tab_ref[3], tab_ref[4], tab_ref[5]
    for hd in range(MLA_HEADS):
        sl = slice(hd * LANES, (hd + 1) * LANES)
        qm_ref[0, hd] = _rope_slab(qm[:, sl], cq_t, sq1, sq2, 16).astype(BF16)
        km_ref[0, hd] = (kn[:, sl] + kr).astype(BF16)
    vt_ref[0] = _dot_nt(wvt_ref[...], ckvn).astype(BF16)


def _proj(x, mod3, g1, tabs, w_in, gq, gkv, w_uq, w_k, w_vt, tm):
    b, s, _ = x.shape
    full = lambda shp: pl.BlockSpec(shp, lambda i, j: (0,) * len(shp))
    return pl.pallas_call(
        _proj_kernel,
        grid=(s // tm, b),
        in_specs=[pl.BlockSpec((1, tm, D_MODEL), lambda i, j: (j, i, 0)),
                  pl.BlockSpec((1, N_MOD, D_MODEL), lambda i, j: (j, 0, 0)),
                  full((1, D_MODEL)),
                  pl.BlockSpec((9, tm, LANES), lambda i, j: (0, i, 0)),
                  full((D_MODEL, IN_COLS)),
                  full((1, MLA_Q_RANK_PAD)),
                  full((1, MLA_KV_RANK)),
                  full((MLA_Q_RANK_PAD, MLA_HEADS * LANES)),
                  full((MLA_KV_RANK, MLA_HEADS * LANES)),
                  full((MLA_WIDTH, MLA_KV_RANK))],
        out_specs=[pl.BlockSpec((1, tm, SWA_WIDTH), lambda i, j: (j, i, 0)),
                   pl.BlockSpec((1, tm, LANES), lambda i, j: (j, i, 0)),
                   pl.BlockSpec((1, tm, LANES), lambda i, j: (j, i, 0)),
                   pl.BlockSpec((1, MLA_HEADS, tm, LANES), lambda i, j: (j, 0, i, 0)),
                   pl.BlockSpec((1, MLA_HEADS, tm, LANES), lambda i, j: (j, 0, i, 0)),
                   pl.BlockSpec((1, MLA_WIDTH, tm), lambda i, j: (j, 0, i))],
        out_shape=[jax.ShapeDtypeStruct((b, s, SWA_WIDTH), BF16),
                   jax.ShapeDtypeStruct((b, s, LANES), BF16),
                   jax.ShapeDtypeStruct((b, s, LANES), BF16),
                   jax.ShapeDtypeStruct((b, MLA_HEADS, s, LANES), BF16),
                   jax.ShapeDtypeStruct((b, MLA_HEADS, s, LANES), BF16),
                   jax.ShapeDtypeStruct((b, MLA_WIDTH, s), BF16)],
        compiler_params=_params(("arbitrary", "arbitrary")),
        name="norm1_proj",
    )(x, mod3, g1, tabs, w_in, gq, gkv, w_uq, w_k, w_vt)


def _swa_kernel(sink_ref, q_ref, kp_ref, kc_ref, kn_ref, vp_ref, vc_ref, vn_ref, g_ref, o_ref, *, nb):
    n = pl.program_id(1)
    k = jnp.concatenate([kp_ref[0], kc_ref[0], kn_ref[0]], axis=0)
    v = jnp.concatenate([vp_ref[0], vc_ref[0], vn_ref[0]], axis=0)
    k_sw = pltpu.roll(k, SWA_HEAD_DIM, 1)
    v_sw = pltpu.roll(v, SWA_HEAD_DIM, 1)
    lane = lax.broadcasted_iota(jnp.int32, k.shape, 1)
    lo = lane < SWA_HEAD_DIM
    zero = jnp.zeros_like(k)
    k_var = ((jnp.where(lo, k, zero), jnp.where(lo, zero, k_sw)),
             (jnp.where(lo, k_sw, zero), jnp.where(lo, zero, k)))
    v_var = ((jnp.where(lo, v, zero), jnp.where(lo, zero, v_sw)),
             (jnp.where(lo, v_sw, zero), jnp.where(lo, zero, v)))

    r = lax.broadcasted_iota(jnp.int32, (BLOCK, 3 * BLOCK), 0)
    j = lax.broadcasted_iota(jnp.int32, (BLOCK, 3 * BLOCK), 1)
    mask = (j >= r) & (j <= r + 2 * BLOCK)
    mask = mask & ((j >= BLOCK) | (n > 0)) & ((j < 2 * BLOCK) | (n < nb - 1))

    slabs = []
    ssq = jnp.zeros((BLOCK, 1), F32)
    group = SWA_HEADS // SWA_KV_HEADS
    for s in range(SWA_WIDTH // LANES):
        q = q_ref[0, :, s * LANES:(s + 1) * LANES]
        o_slab = jnp.zeros((BLOCK, LANES), F32)
        for half in range(2):
            head = 2 * s + half
            kv = head // group
            sc = _dot_nt(q, k_var[kv][half])
            sc = jnp.where(mask, sc, NEG)
            snk = sink_ref[0, head]
            m = jnp.maximum(jnp.max(sc, axis=-1, keepdims=True), snk)
            p = jnp.exp(sc - m)
            denom = jnp.sum(p, axis=-1, keepdims=True) + jnp.exp(snk - m)
            o_slab = o_slab + _dot(p.astype(BF16), v_var[kv][half]) * (1.0 / denom)
        slabs.append(o_slab)
        ssq = ssq + jnp.sum(o_slab * o_slab, axis=-1, keepdims=True)
    scale = lax.rsqrt(ssq * (1.0 / SWA_WIDTH) + EPS)
    for s, o_slab in enumerate(slabs):
        sl = slice(s * LANES, (s + 1) * LANES)
        o_ref[0, :, sl] = (o_slab * scale * g_ref[:, sl]).astype(BF16)


def _swa(sink, qa, ka, va, g_out):
    b, s, _ = qa.shape
    nb = s // BLOCK
    kv_spec = lambda f: pl.BlockSpec((1, BLOCK, LANES), f)
    prev = lambda i, n: (i, jnp.maximum(n - 1, 0), 0)
    cur = lambda i, n: (i, n, 0)
    nxt = lambda i, n: (i, jnp.minimum(n + 1, nb - 1), 0)
    return pl.pallas_call(
        functools.partial(_swa_kernel, nb=nb),
        grid=(b, nb),
        in_specs=[pl.BlockSpec(memory_space=pltpu.SMEM),
                  pl.BlockSpec((1, BLOCK, SWA_WIDTH), cur),
                  kv_spec(prev), kv_spec(cur), kv_spec(nxt),
                  kv_spec(prev), kv_spec(cur), kv_spec(nxt),
                  pl.BlockSpec((1, SWA_WIDTH), lambda i, n: (0, 0))],
        out_specs=pl.BlockSpec((1, BLOCK, SWA_WIDTH), cur),
        out_shape=jax.ShapeDtypeStruct((b, s, SWA_WIDTH), BF16),
        compiler_params=_params(("arbitrary", "arbitrary")),
        name="swa_attn",
    )(sink, qa, ka, ka, ka, va, va, va, g_out)


def _mla_kernel(q_ref, k_ref, vt_ref, o_ref, s0_ref, s1_ref, *, tq, tk):
    s_len = k_ref.shape[2]
    n_tiles = s_len // tq
    n_chunks = s_len // tk
    ones = jnp.ones((16, tk), BF16)

    def pass_a_chunk(t_off, hd, c, s_ref, m8):
        q = q_ref[0, hd, pl.ds(t_off, tq), :]
        st = _dot_nt(k_ref[0, hd, c * tk:(c + 1) * tk, :], q)
        s_ref[hd, c * tk:(c + 1) * tk, :] = st
        return jnp.maximum(m8, jnp.max(st.reshape(tk // 8, 8, tq), axis=0))

    def pass_b_chunk(hd, c, s_ref, m, acc):
        p = jnp.exp2(s_ref[hd, c * tk:(c + 1) * tk, :] - m).astype(BF16)
        v = vt_ref[0, hd * MLA_V:(hd + 1) * MLA_V, c * tk:(c + 1) * tk]
        vext = jnp.concatenate([v, ones], axis=0)
        return acc + _dot(vext, p)

    def stage(t_a, sa_ref, t_b, sb_ref, m_b):
        a_off = pl.multiple_of(t_a * tq, tq)
        m_a, outs = [], []
        for hd in range(2):
            m8 = jnp.full((8, tq), NEG, F32)
            acc = jnp.zeros((MLA_V + 16, tq), F32)
            for c in range(n_chunks):
                m8 = pass_a_chunk(a_off, hd, c, sa_ref, m8)
                if t_b is not None:
                    acc = pass_b_chunk(hd, c, sb_ref, m_b[hd], acc)
            m_a.append(jnp.max(m8, axis=0, keepdims=True))
            if t_b is not None:
                outs.append(acc[:MLA_V] * (1.0 / acc[MLA_V:MLA_V + 1]))
        if t_b is not None:
            b_off = pl.multiple_of(t_b * tq, tq)
            o_ref[0, pl.ds(b_off, tq), :] = jnp.concatenate(outs, axis=0).T
        return tuple(m_a)

    def body(j, m_even):
        m_odd = stage(2 * j + 1, s1_ref, 2 * j, s0_ref, m_even)
        return stage(jnp.minimum(2 * j + 2, n_tiles - 1), s0_ref, 2 * j + 1, s1_ref, m_odd)

    m0 = stage(0, s0_ref, None, None, None)
    lax.fori_loop(0, n_tiles // 2, body, m0)


def _mla(qm, km, vt, tq, tk):
    b, h, s, _ = qm.shape
    return pl.pallas_call(
        functools.partial(_mla_kernel, tq=tq, tk=tk),
        grid=(b, h // 2),
        in_specs=[pl.BlockSpec((1, 2, s, LANES), lambda i, p: (i, p, 0, 0)),
                  pl.BlockSpec((1, 2, s, LANES), lambda i, p: (i, p, 0, 0)),
                  pl.BlockSpec((1, 2 * MLA_V, s), lambda i, p: (i, p, 0))],
        out_specs=pl.BlockSpec((1, s, LANES), lambda i, p: (i, 0, p)),
        out_shape=jax.ShapeDtypeStruct((b, s, MLA_WIDTH), F32),
        scratch_shapes=[pltpu.VMEM((2, s, tq), F32), pltpu.VMEM((2, s, tq), F32)],
        compiler_params=_params(("arbitrary", "arbitrary")),
        name="mla_attn",
    )(qm, km, vt)


def _out_kernel(ma_ref, ob_ref, x_ref, mod_ref, gb_ref, wo_ref, g2_ref, wr_ref, br_ref, tri_ref,
                x1_ref, h2_ref, rt_ref, rtm_ref, cnt_ref, carry_ref):
    first = (pl.program_id(0) == 0) & (pl.program_id(1) == 0)

    @pl.when(first)
    def _():
        carry_ref[...] = jnp.zeros_like(carry_ref)

    ob = ob_ref[0]
    mb = (ob * lax.rsqrt(jnp.mean(ob * ob, axis=-1, keepdims=True) + EPS) * gb_ref[...]).astype(BF16)
    y = _dot(ma_ref[0], wo_ref[0:SWA_WIDTH, :]) + _dot(mb, wo_ref[SWA_WIDTH:, :])
    x1 = x_ref[0] + mod_ref[0, 2:3, :] * y
    x1_ref[0] = x1
    h = x1 * lax.rsqrt(jnp.mean(x1 * x1, axis=-1, keepdims=True) + EPS) * g2_ref[...]
    h = h * (1.0 + mod_ref[0, 4:5, :]) + mod_ref[0, 3:4, :]
    h2_ref[0] = h
    h_hi, h_lo = _split_bf16(h)

    tm = h.shape[0]
    l_hi = _dot_nt(wr_ref[...], h_hi)
    l_lo = _dot_nt(wr_ref[0:ROUTE_ROWS, :], h_lo)
    logit = l_hi[0:ROUTE_ROWS] + l_hi[ROUTE_ROWS:] + l_lo + br_ref[...]
    row8 = lax.broadcasted_iota(jnp.int32, (8, tm), 0)
    gl = jnp.where(row8 < N_GROUPS, logit[0:8], NEG)
    gmax = jnp.max(gl, axis=0, keepdims=True)
    gsum = jnp.sum(jnp.exp(gl - gmax), axis=0, keepdims=True)
    g_idx = jnp.min(jnp.where(gl == gmax, row8, 8), axis=0, keepdims=True)
    g_w = 1.0 / gsum
    ec = logit[8:16]
    for g in range(1, N_GROUPS):
        ec = jnp.where(g_idx == g, logit[8 + 8 * g:16 + 8 * g], ec)
    m1 = jnp.max(ec, axis=0, keepdims=True)
    i1 = jnp.min(jnp.where(ec == m1, row8, 8), axis=0, keepdims=True)
    ec2 = jnp.where(row8 == i1, NEG, ec)
    m2 = jnp.max(ec2, axis=0, keepdims=True)
    i2 = jnp.min(jnp.where(ec2 == m2, row8, 8), axis=0, keepdims=True)
    t = jnp.exp(m2 - m1)
    w1 = g_w / (1.0 + t)
    w2 = g_w * t / (1.0 + t)
    e1 = g_idx * EXPERTS_PER_GROUP + i1
    e2 = g_idx * EXPERTS_PER_GROUP + i2

    rows = lax.broadcasted_iota(jnp.int32, (N_EXPERTS, tm), 0)
    oh1 = (rows == e1).astype(F32)
    oh2 = (rows == e2).astype(F32)
    both = oh1 + oh2
    incl = _dot(both.astype(BF16), tri_ref[...])
    before = carry_ref[:, 0:1] + incl - both
    r1 = jnp.sum(oh1 * before, axis=0, keepdims=True)
    r2 = jnp.sum(oh2 * before, axis=0, keepdims=True)
    total = carry_ref[:, 0:1] + incl[:, tm - 1:tm]
    carry_ref[...] = jnp.broadcast_to(total, carry_ref.shape)
    cnt_ref[...] = jnp.broadcast_to(total, cnt_ref.shape)

    route = jnp.concatenate([e1.astype(F32), e2.astype(F32), w1, w2, r1, r2, jnp.zeros((2, tm), F32)], axis=0)
    rt_ref[...] = route
    rtm_ref[...] = jnp.concatenate([route, jnp.zeros((LANES - 8, tm), F32)], axis=0).T


def _outproj(mixa, ob, x, mod3, g_mla, w_out, g2, wr, br, tm):
    b, s, _ = x.shape
    nt = s // tm
    tri = jnp.triu(jnp.ones((tm, tm), F32)).astype(BF16)
    full = lambda shp: pl.BlockSpec(shp, lambda i, j: (0,) * len(shp))
    tok = lambda w: pl.BlockSpec((1, tm, w), lambda i, j: (i, j, 0))
    return pl.pallas_call(
        _out_kernel,
        grid=(b, nt),
        in_specs=[tok(SWA_WIDTH), tok(MLA_WIDTH), tok(D_MODEL),
                  pl.BlockSpec((1, N_MOD, D_MODEL), lambda i, j: (i, 0, 0)),
                  full((1, MLA_WIDTH)), full((D_MODEL, D_MODEL)), full((1, D_MODEL)),
                  full((2 * ROUTE_ROWS, D_MODEL)), full((ROUTE_ROWS, 1)), full((tm, tm))],
        out_specs=[tok(D_MODEL), tok(D_MODEL),
                   pl.BlockSpec((8, tm), lambda i, j: (0, i * nt + j)),
                   pl.BlockSpec((tm, LANES), lambda i, j: (i * nt + j, 0)),
                   full((N_EXPERTS, LANES))],
        out_shape=[jax.ShapeDtypeStruct((b, s, D_MODEL), F32),
                   jax.ShapeDtypeStruct((b, s, D_MODEL), F32),
                   jax.ShapeDtypeStruct((8, b * s), F32),
                   jax.ShapeDtypeStruct((b * s, LANES), F32),
                   jax.ShapeDtypeStruct((N_EXPERTS, LANES), F32)],
        scratch_shapes=[pltpu.VMEM((N_EXPERTS, LANES), F32)],
        compiler_params=_params(("arbitrary", "arbitrary")),
        name="outproj_router",
    )(mixa, ob, x, mod3, g_mla, w_out, g2, wr, br, tri)


def _row_copy(src_ref, src_row, dst_ref, dst_row, sem):
    return pltpu.make_async_copy(src_ref.at[pl.ds(src_row, 1)], dst_ref.at[pl.ds(dst_row, 1)], sem)


def _dispatch_kernel(slot_ref, zst_ref, h_ref, xs_ref, zbuf, sem, zsem, *, tmd):
    i = pl.program_id(0)

    @pl.when(i == 0)
    def _():
        zbuf[...] = jnp.zeros_like(zbuf)
        for phase in ("start", "wait"):
            for e in range(N_EXPERTS):
                z = zst_ref[e]

                @pl.when(z >= 0)
                def _():
                    zrow = pl.multiple_of(z, EXPERT_TILE)
                    cp = pltpu.make_async_copy(zbuf, xs_ref.at[pl.ds(zrow, EXPERT_TILE)], zsem)
                    cp.start() if phase == "start" else cp.wait()

    def issue(r, carry):
        for k in range(2):
            _row_copy(h_ref, r, xs_ref, slot_ref[0, 0, 2 * r + k], sem).start(priority=k)
        return carry

    def drain(r, carry):
        _row_copy(h_ref, 0, xs_ref, 0, sem).wait()
        return carry

    lax.fori_loop(0, tmd, issue, 0, unroll=4)
    lax.fori_loop(0, 2 * tmd, drain, 0, unroll=8)


def _dispatch(slots3, zstart, h2, n_slots, tmd):
    t = h2.shape[0]
    return pl.pallas_call(
        functools.partial(_dispatch_kernel, tmd=tmd),
        grid=(t // tmd,),
        in_specs=[pl.BlockSpec((1, 1, 2 * tmd), lambda i: (i, 0, 0), memory_space=pltpu.SMEM),
                  pl.BlockSpec(memory_space=pltpu.SMEM),
                  pl.BlockSpec((tmd, D_MODEL), lambda i: (i, 0))],
        out_specs=pl.BlockSpec(memory_space=pl.ANY),
        out_shape=jax.ShapeDtypeStruct((n_slots, D_MODEL), F32),
        scratch_shapes=[pltpu.VMEM((EXPERT_TILE, D_MODEL), F32),
                        pltpu.SemaphoreType.DMA(()), pltpu.SemaphoreType.DMA(())],
        compiler_params=_params(("arbitrary",)),
        name="moe_dispatch",
    )(slots3, zstart, h2)


def _expert_kernel(te_ref, nt_ref, xs_ref, wg_ref, wu_ref, wd_ref, y_ref, wgu_buf, wd_buf):
    i = pl.program_id(0)

    @pl.when(i < nt_ref[0])
    def _():
        changed = (i == 0) | (te_ref[i] != te_ref[jnp.maximum(i - 1, 0)])

        @pl.when(changed)
        def _():
            wgu_buf[:, 0:D_EXPERT] = wg_ref[0].astype(BF16)
            wgu_buf[:, D_EXPERT:] = wu_ref[0].astype(BF16)
            wd_buf[...] = wd_ref[0].astype(BF16)

        gu = _dot(xs_ref[...].astype(BF16), wgu_buf[...])
        g = gu[:, 0:D_EXPERT]
        hid = (g * (1.0 / (1.0 + jnp.exp(-g))) * gu[:, D_EXPERT:]).astype(BF16)
        y_ref[...] = _dot(hid, wd_buf[...])


def _experts(tile_expert, n_tiles, xs, wg, wu, wd):
    max_tiles = xs.shape[0] // EXPERT_TILE
    row = lambda i, te, nt: (jnp.minimum(i, nt[0] - 1), 0)
    wsel = lambda i, te, nt: (te[i], 0, 0)
    return pl.pallas_call(
        _expert_kernel,
        grid_spec=pltpu.PrefetchScalarGridSpec(
            num_scalar_prefetch=2,
            grid=(max_tiles,),
            in_specs=[pl.BlockSpec((EXPERT_TILE, D_MODEL), row),
                      pl.BlockSpec((1, D_MODEL, D_EXPERT), wsel),
                      pl.BlockSpec((1, D_MODEL, D_EXPERT), wsel),
                      pl.BlockSpec((1, D_EXPERT, D_MODEL), wsel)],
            out_specs=pl.BlockSpec((EXPERT_TILE, D_MODEL), row),
            scratch_shapes=[pltpu.VMEM((D_MODEL, 2 * D_EXPERT), BF16),
                            pltpu.VMEM((D_EXPERT, D_MODEL), BF16)]),
        out_shape=jax.ShapeDtypeStruct(xs.shape, F32),
        compiler_params=_params(("arbitrary",)),
        name="moe_experts",
    )(tile_expert, n_tiles, xs, wg, wu, wd)


def _combine_kernel(slot_ref, y_ref, rt_ref, x1_ref, mod_ref, gf_ref, o_ref, ybuf, sem, *, tmc):
    def issue(r, carry):
        for k in range(2):
            _row_copy(y_ref, slot_ref[0, 0, 2 * r + k], ybuf.at[k], r, sem).start(priority=k)
        return carry

    def drain(r, carry):
        _row_copy(y_ref, 0, ybuf.at[0], 0, sem).wait()
        return carry

    lax.fori_loop(0, tmc, issue, 0, unroll=4)
    lax.fori_loop(0, 2 * tmc, drain, 0, unroll=8)
    moe = rt_ref[:, 2:3] * ybuf[0] + rt_ref[:, 3:4] * ybuf[1]
    v = x1_ref[...] + mod_ref[0, 5:6, :] * moe
    o_ref[...] = v * lax.rsqrt(jnp.mean(v * v, axis=-1, keepdims=True) + EPS) * gf_ref[...]


def _combine(slots3, y, rtm, x1, mod3, gf, seq, tmc):
    t = x1.shape[0]
    per_b = seq // tmc
    tok = lambda w: pl.BlockSpec((tmc, w), lambda i: (i, 0))
    return pl.pallas_call(
        functools.partial(_combine_kernel, tmc=tmc),
        grid=(t // tmc,),
        in_specs=[pl.BlockSpec((1, 1, 2 * tmc), lambda i: (i, 0, 0), memory_space=pltpu.SMEM),
                  pl.BlockSpec(memory_space=pl.ANY),
                  tok(LANES), tok(D_MODEL),
                  pl.BlockSpec((1, N_MOD, D_MODEL), lambda i: (i // per_b, 0, 0)),
                  pl.BlockSpec((1, D_MODEL), lambda i: (0, 0))],
        out_specs=tok(D_MODEL),
        out_shape=jax.ShapeDtypeStruct((t, D_MODEL), F32),
        scratch_shapes=[pltpu.VMEM((2, tmc, D_MODEL), F32), pltpu.SemaphoreType.DMA(())],
        compiler_params=_params(("arbitrary",)),
        name="moe_combine_final",
    )(slots3, y, rtm, x1, mod3, gf)


def _plan_slots(route_t, counts, n_tokens):
    e = route_t[0:2].astype(jnp.int32)
    rank = route_t[4:6].astype(jnp.int32)
    cnt = counts[:, 0].astype(jnp.int32)
    tiles = (cnt + EXPERT_TILE - 1) // EXPERT_TILE
    ids = jnp.arange(N_EXPERTS, dtype=jnp.int32)
    tile_end = jnp.sum(jnp.where(ids[:, None] <= ids[None, :], tiles[:, None], 0), axis=0)
    base = (tile_end - tiles) * EXPERT_TILE
    base_of = jnp.sum(jnp.where(e[:, :, None] == ids[None, None, :], base[None, None, :], 0), axis=-1)
    slots = (base_of + rank).T
    max_tiles = (2 * n_tokens + N_EXPERTS * (EXPERT_TILE - 1)) // EXPERT_TILE
    tile_ids = jnp.arange(max_tiles, dtype=jnp.int32)
    tile_expert = jnp.minimum(jnp.sum((tile_ids[:, None] >= tile_end[None, :]).astype(jnp.int32), axis=1),
                              N_EXPERTS - 1)
    zstart = jnp.where(tiles > 0, (tile_end - 1) * EXPERT_TILE, -1).astype(jnp.int32)
    return slots, tile_expert, tile_end[-1:].astype(jnp.int32), zstart, max_tiles * EXPERT_TILE


def _rope_tables(seq, dim):
    inv = 1.0 / (ROPE_THETA ** (jnp.arange(0, dim, 2, dtype=F32) / dim))
    ang = jnp.arange(seq, dtype=F32)[:, None] * inv[None, :]
    return jnp.cos(ang), jnp.sin(ang)


def _tables(seq):
    ca, sa = _rope_tables(seq, SWA_HEAD_DIM)
    cb, sb = _rope_tables(seq, MLA_ROPE)
    z32 = jnp.zeros_like(sa)
    z16 = jnp.zeros_like(sb)
    z64 = jnp.zeros((seq, 64), F32)
    one64 = jnp.ones((seq, 64), F32)
    cat = lambda parts: jnp.concatenate(parts, axis=1)
    scale = float((MLA_NOPE + MLA_ROPE) ** -0.5 * np.log2(np.e))
    swa = [cat([ca] * 4), cat([z32, sa] * 2), cat([-sa, z32] * 2)]
    mq = [cat([one64, cb, cb, z32]) * scale, cat([z64, z16, sb, z32]) * scale, cat([z64, -sb, z16, z32]) * scale]
    mk = [cat([z64, cb, cb, z32]), cat([z64, z16, sb, z32]), cat([z64, -sb, z16, z32])]
    return jnp.stack(swa + mq + mk, axis=0)


def _layout_weights(w_in, w_uq, w_ukv, g_q):
    qa, ka, va, cq, ckv, kr = jnp.split(w_in, np.cumsum([512, 128, 128, 192, 128, 32])[:-1].tolist(), axis=1)
    zc = lambda n: jnp.zeros((D_MODEL, n), F32)
    w_in_l = jnp.concatenate([qa * (SWA_HEAD_DIM ** -0.5), ka, va, ckv, cq, zc(64), zc(64), kr, zc(32)], axis=1)
    uq = w_uq.reshape(MLA_Q_RANK, MLA_HEADS, MLA_NOPE + MLA_ROPE)
    uq = jnp.pad(uq, ((0, MLA_Q_RANK_PAD - MLA_Q_RANK), (0, 0), (0, LANES - MLA_NOPE - MLA_ROPE)))
    ukv = w_ukv.reshape(MLA_KV_RANK, MLA_HEADS, MLA_NOPE + MLA_V)
    wk = jnp.pad(ukv[:, :, :MLA_NOPE], ((0, 0), (0, 0), (0, LANES - MLA_NOPE)))
    wvt = ukv[:, :, MLA_NOPE:].reshape(MLA_KV_RANK, MLA_WIDTH).T
    gq = jnp.pad(g_q, (0, MLA_Q_RANK_PAD - MLA_Q_RANK))[None, :]
    return (w_in_l.astype(BF16), uq.reshape(MLA_Q_RANK_PAD, MLA_HEADS * LANES).astype(BF16),
            wk.reshape(MLA_KV_RANK, MLA_HEADS * LANES).astype(BF16), wvt.astype(BF16), gq)


def _layout_router(w_rg, b_rg, w_re, b_re):
    w = jnp.zeros((ROUTE_ROWS, D_MODEL), F32)
    w = w.at[0:N_GROUPS].set(w_rg.T).at[8:8 + N_EXPERTS].set(w_re.T)
    bias = jnp.zeros((ROUTE_ROWS,), F32).at[0:N_GROUPS].set(b_rg).at[8:8 + N_EXPERTS].set(b_re)
    hi = w.astype(BF16)
    lo = (w - hi.astype(F32)).astype(BF16)
    return jnp.concatenate([hi, lo], axis=0), bias[:, None]


def kernel(x, c, w_ada, b_ada, g_norm1, w_in, g_q_lora, w_uq, g_kv_lora, w_ukv, sink, g_out_swa, g_out_mla, w_out, g_norm2, w_router_group, b_router_group, w_router_expert, b_router_expert, w_exp_gate, w_exp_up, w_exp_down, g_final):
    b, s, d = x.shape
    assert d == D_MODEL and s % 512 == 0 and b <= 8
    assert w_ada.shape[0] == 1, "single layer"

    c8 = jnp.pad(c, ((0, 8 - b), (0, 0)))
    mod3 = _ada(c8, w_ada[0], b_ada[0][None, :]).reshape(8, N_MOD, D_MODEL)

    w_in_l, w_uq_l, w_k_l, w_vt_l, gq = _layout_weights(w_in[0], w_uq[0], w_ukv[0], g_q_lora[0])
    qa, ka, va, qm, km, vt = _proj(x, mod3, g_norm1[0][None, :], _tables(s), w_in_l, gq,
                                   g_kv_lora[0][None, :], w_uq_l, w_k_l, w_vt_l, tm=256)

    mixa = _swa(sink[0][None, :], qa, ka, va, g_out_swa[0][None, :])
    ob = _mla(qm, km, vt, tq=256, tk=256)

    wr, br = _layout_router(w_router_group[0], b_router_group[0], w_router_expert[0], b_router_expert[0])
    x1, h2, route_t, rtm, counts = _outproj(mixa, ob, x, mod3, g_out_mla[0][None, :], w_out[0].astype(BF16),
                                            g_norm2[0][None, :], wr, br, tm=256)

    t = b * s
    slots, tile_expert, n_tiles, zstart, n_slots = _plan_slots(route_t, counts, t)
    tmd, tmc = 512, 256
    xs = _dispatch(slots.reshape(t // tmd, 1, 2 * tmd), zstart, h2.reshape(t, d), n_slots, tmd)
    y = _experts(tile_expert, n_tiles, xs, w_exp_gate[0], w_exp_up[0], w_exp_down[0])
    out = _combine(slots.reshape(t // tmc, 1, 2 * tmc), y, rtm, x1.reshape(t, d), mod3, g_final[None, :], s, tmc)
    return out.reshape(b, s, d)
```

```python
import functools

import jax
import jax.numpy as jnp
import numpy as np
from jax import lax
from jax.experimental import pallas as pl
from jax.experimental.pallas import tpu as pltpu

F32 = jnp.float32
BF16 = jnp.bfloat16

D_MODEL = 1024
EPS = 1e-6
ROPE_THETA = 10000.0
NEG = -1e30
LANES = 128

BLOCK = 128
SWA_HEADS = 8
SWA_KV_HEADS = 2
SWA_HEAD_DIM = 64
SWA_WIDTH = SWA_HEADS * SWA_HEAD_DIM

MLA_HEADS = 8
MLA_NOPE = 64
MLA_ROPE = 32
MLA_V = 64
MLA_Q_RANK = 192
MLA_Q_RANK_PAD = 256
MLA_KV_RANK = 128
MLA_WIDTH = MLA_HEADS * MLA_V

N_GROUPS = 4
EXPERTS_PER_GROUP = 8
N_EXPERTS = 32
D_EXPERT = 256
N_MOD = 6

COL_QA = 0
COL_KA = 512
COL_VA = 640
COL_CKV = 768
COL_CQ = 896
COL_KR = 1152
IN_COLS = 1280

EXPERT_TILE = 256
ROUTE_ROWS = 48

VMEM_LIMIT = 56 * 1024 * 1024

NT_DIMS = (((1,), (1,)), ((), ()))


def _dot(a, b):
    return jnp.dot(a, b, preferred_element_type=F32)


def _dot_nt(a, b):
    return lax.dot_general(a, b, NT_DIMS, preferred_element_type=F32)


def _split_bf16(a):
    hi = a.astype(BF16)
    lo = (a - hi.astype(F32)).astype(BF16)
    return hi, lo


def _params(sem):
    return pltpu.CompilerParams(dimension_semantics=sem, vmem_limit_bytes=VMEM_LIMIT)


def _ada_kernel(c_ref, w_ref, b_ref, o_ref):
    c = c_ref[...]
    a = c * (1.0 / (1.0 + jnp.exp(-c)))
    a_hi, a_lo = _split_bf16(a)
    w_hi, w_lo = _split_bf16(w_ref[...])
    o_ref[...] = _dot(a_hi, w_hi) + _dot(a_lo, w_hi) + _dot(a_hi, w_lo) + b_ref[...]


def _ada(c8, w_ada, b_ada):
    n = w_ada.shape[1]
    tn = 1024
    return pl.pallas_call(
        _ada_kernel,
        grid=(n // tn,),
        in_specs=[pl.BlockSpec((8, D_MODEL), lambda j: (0, 0)),
                  pl.BlockSpec((D_MODEL, tn), lambda j: (0, j)),
                  pl.BlockSpec((1, tn), lambda j: (0, j))],
        out_specs=pl.BlockSpec((8, tn), lambda j: (0, j)),
        out_shape=jax.ShapeDtypeStruct((8, n), F32),
        compiler_params=_params(("arbitrary",)),
        name="ada_mod",
    )(c8, w_ada, b_ada)


def _rope_slab(x, c, s1, s2, sh):
    return x * c + pltpu.roll(x, sh, 1) * s1 + pltpu.roll(x, LANES - sh, 1) * s2


def _proj_kernel(x_ref, mod_ref, g1_ref, tab_ref, win_ref, gq_ref, gkv_ref, wuq_ref, wk_ref, wvt_ref,
                 qa_ref, ka_ref, va_ref, qm_ref, km_ref, vt_ref):
    x = x_ref[0]
    ms = jnp.mean(x * x, axis=-1, keepdims=True)
    h = x * lax.rsqrt(ms + EPS) * g1_ref[...]
    h = h * (1.0 + mod_ref[0, 1:2, :]) + mod_ref[0, 0:1, :]
    proj = _dot(h.astype(BF16), win_ref[...])

    ca, sa1, sa2 = tab_ref[0], tab_ref[1], tab_ref[2]
    for s in range(SWA_WIDTH // LANES):
        q = proj[:, COL_QA + s * LANES:COL_QA + (s + 1) * LANES]
        qa_ref[0, :, s * LANES:(s + 1) * LANES] = _rope_slab(q, ca, sa1, sa2, 32).astype(BF16)
    ka_ref[0] = _rope_slab(proj[:, COL_KA:COL_KA + LANES], ca, sa1, sa2, 32).astype(BF16)
    va_ref[0] = proj[:, COL_VA:COL_VA + LANES].astype(BF16)

    ckv = proj[:, COL_CKV:COL_CKV + MLA_KV_RANK]
    ckvn = (ckv * lax.rsqrt(jnp.mean(ckv * ckv, axis=-1, keepdims=True) + EPS) * gkv_ref[...]).astype(BF16)
    cq = proj[:, COL_CQ:COL_CQ + MLA_Q_RANK_PAD]
    cq_ms = jnp.sum(cq * cq, axis=-1, keepdims=True) * (1.0 / MLA_Q_RANK)
    cqn = (cq * lax.rsqrt(cq_ms + EPS) * gq_ref[...]).astype(BF16)

    kr = _rope_slab(proj[:, COL_KR:COL_KR + LANES], tab_ref[6], tab_ref[7], tab_ref[8], 16)
    qm = _dot(cqn, wuq_ref[...])
    kn = _dot(ckvn, wk_ref[...])
    cq_t, sq1, sq2 = tab_ref[3], tab_ref[4], tab_ref[5]
    for hd in range(MLA_HEADS):
        sl = slice(hd * LANES, (hd + 1) * LANES)
        qm_ref[0, hd] = _rope_slab(qm[:, sl], cq_t, sq1, sq2, 16).astype(BF16)
        km_ref[0, hd] = (kn[:, sl] + kr).astype(BF16)
    vt_ref[0] = _dot_nt(wvt_ref[...], ckvn).astype(BF16)


def _proj(x, mod3, g1, tabs, w_in, gq, gkv, w_uq, w_k, w_vt, tm):
    b, s, _ = x.shape
    full = lambda shp: pl.BlockSpec(shp, lambda i, j: (0,) * len(shp))
    return pl.pallas_call(
        _proj_kernel,
        grid=(s // tm, b),
        in_specs=[pl.BlockSpec((1, tm, D_MODEL), lambda i, j: (j, i, 0)),
                  pl.BlockSpec((1, N_MOD, D_MODEL), lambda i, j: (j, 0, 0)),
                  full((1, D_MODEL)),
                  pl.BlockSpec((9, tm, LANES), lambda i, j: (0, i, 0)),
                  full((D_MODEL, IN_COLS)),
                  full((1, MLA_Q_RANK_PAD)),
                  full((1, MLA_KV_RANK)),
                  full((MLA_Q_RANK_PAD, MLA_HEADS * LANES)),
                  full((MLA_KV_RANK, MLA_HEADS * LANES)),
                  full((MLA_WIDTH, MLA_KV_RANK))],
        out_specs=[pl.BlockSpec((1, tm, SWA_WIDTH), lambda i, j: (j, i, 0)),
                   pl.BlockSpec((1, tm, LANES), lambda i, j: (j, i, 0)),
                   pl.BlockSpec((1, tm, LANES), lambda i, j: (j, i, 0)),
                   pl.BlockSpec((1, MLA_HEADS, tm, LANES), lambda i, j: (j, 0, i, 0)),
                   pl.BlockSpec((1, MLA_HEADS, tm, LANES), lambda i, j: (j, 0, i, 0)),
                   pl.BlockSpec((1, MLA_WIDTH, tm), lambda i, j: (j, 0, i))],
        out_shape=[jax.ShapeDtypeStruct((b, s, SWA_WIDTH), BF16),
                   jax.ShapeDtypeStruct((b, s, LANES), BF16),
                   jax.ShapeDtypeStruct((b, s, LANES), BF16),
                   jax.ShapeDtypeStruct((b, MLA_HEADS, s, LANES), BF16),
                   jax.ShapeDtypeStruct((b, MLA_HEADS, s, LANES), BF16),
                   jax.ShapeDtypeStruct((b, MLA_WIDTH, s), BF16)],
        compiler_params=_params(("arbitrary", "arbitrary")),
        name="norm1_proj",
    )(x, mod3, g1, tabs, w_in, gq, gkv, w_uq, w_k, w_vt)


def _swa_kernel(sink_ref, q_ref, kp_ref, kc_ref, kn_ref, vp_ref, vc_ref, vn_ref, g_ref, o_ref, *, nb):
    n = pl.program_id(1)
    k = jnp.concatenate([kp_ref[0], kc_ref[0], kn_ref[0]], axis=0)
    v = jnp.concatenate([vp_ref[0], vc_ref[0], vn_ref[0]], axis=0)
    k_sw = pltpu.roll(k, SWA_HEAD_DIM, 1)
    v_sw = pltpu.roll(v, SWA_HEAD_DIM, 1)
    lane = lax.broadcasted_iota(jnp.int32, k.shape, 1)
    lo = lane < SWA_HEAD_DIM
    zero = jnp.zeros_like(k)
    k_var = ((jnp.where(lo, k, zero), jnp.where(lo, zero, k_sw)),
             (jnp.where(lo, k_sw, zero), jnp.where(lo, zero, k)))
    v_var = ((jnp.where(lo, v, zero), jnp.where(lo, zero, v_sw)),
             (jnp.where(lo, v_sw, zero), jnp.where(lo, zero, v)))

    r = lax.broadcasted_iota(jnp.int32, (BLOCK, 3 * BLOCK), 0)
    j = lax.broadcasted_iota(jnp.int32, (BLOCK, 3 * BLOCK), 1)
    mask = (j >= r) & (j <= r + 2 * BLOCK)
    mask = mask & ((j >= BLOCK) | (n > 0)) & ((j < 2 * BLOCK) | (n < nb - 1))

    group = SWA_HEADS // SWA_KV_HEADS

    def scores(head):
        s, half = divmod(head, 2)
        q = q_ref[0, :, s * LANES:(s + 1) * LANES]
        return _dot_nt(q, k_var[head // group][half])

    ahead = 8
    pending = [scores(h) for h in range(ahead)]
    slabs = []
    ssq = jnp.zeros((BLOCK, 1), F32)
    for head in range(SWA_HEADS):
        if head + ahead < SWA_HEADS:
            pending.append(scores(head + ahead))
        sc = jnp.where(mask, pending[head], NEG)
        snk = sink_ref[0, head]
        m = jnp.maximum(jnp.max(sc, axis=-1, keepdims=True), snk)
        p = jnp.exp(sc - m)
        denom = jnp.sum(p, axis=-1, keepdims=True) + jnp.exp(snk - m)
        o_head = _dot(p.astype(BF16), v_var[head // group][head % 2]) * (1.0 / denom)
        if head % 2 == 0:
            o_slab = o_head
        else:
            o_slab = o_slab + o_head
            slabs.append(o_slab)
            ssq = ssq + jnp.sum(o_slab * o_slab, axis=-1, keepdims=True)
    scale = lax.rsqrt(ssq * (1.0 / SWA_WIDTH) + EPS)
    for s, o_slab in enumerate(slabs):
        sl = slice(s * LANES, (s + 1) * LANES)
        o_ref[0, :, sl] = (o_slab * scale * g_ref[:, sl]).astype(BF16)


def _swa(sink, qa, ka, va, g_out):
    b, s, _ = qa.shape
    nb = s // BLOCK
    kv_spec = lambda f: pl.BlockSpec((1, BLOCK, LANES), f)
    prev = lambda i, n: (i, jnp.maximum(n - 1, 0), 0)
    cur = lambda i, n: (i, n, 0)
    nxt = lambda i, n: (i, jnp.minimum(n + 1, nb - 1), 0)
    return pl.pallas_call(
        functools.partial(_swa_kernel, nb=nb),
        grid=(b, nb),
        in_specs=[pl.BlockSpec(memory_space=pltpu.SMEM),
                  pl.BlockSpec((1, BLOCK, SWA_WIDTH), cur),
                  kv_spec(prev), kv_spec(cur), kv_spec(nxt),
                  kv_spec(prev), kv_spec(cur), kv_spec(nxt),
                  pl.BlockSpec((1, SWA_WIDTH), lambda i, n: (0, 0))],
        out_specs=pl.BlockSpec((1, BLOCK, SWA_WIDTH), cur),
        out_shape=jax.ShapeDtypeStruct((b, s, SWA_WIDTH), BF16),
        compiler_params=_params(("arbitrary", "arbitrary")),
        name="swa_attn",
    )(sink, qa, ka, ka, ka, va, va, va, g_out)


def _mla_kernel(q_ref, k_ref, vt_ref, o_ref, s0_ref, s1_ref, *, tq, tk):
    s_len = k_ref.shape[2]
    n_tiles = s_len // tq
    n_chunks = s_len // tk
    ones = jnp.ones((16, tk), BF16)

    def pass_a_chunk(t_off, hd, c, s_ref, m8):
        q = q_ref[0, hd, pl.ds(t_off, tq), :]
        st = _dot_nt(k_ref[0, hd, c * tk:(c + 1) * tk, :], q)
        s_ref[hd, c * tk:(c + 1) * tk, :] = st
        return jnp.maximum(m8, jnp.max(st.reshape(tk // 8, 8, tq), axis=0))

    def pass_b_chunk(hd, c, s_ref, m, acc):
        p = jnp.exp2(s_ref[hd, c * tk:(c + 1) * tk, :] - m).astype(BF16)
        v = vt_ref[0, hd * MLA_V:(hd + 1) * MLA_V, c * tk:(c + 1) * tk]
        vext = jnp.concatenate([v, ones], axis=0)
        return acc + _dot(vext, p)

    def stage(t_a, sa_ref, t_b, sb_ref, m_b):
        a_off = pl.multiple_of(t_a * tq, tq)
        m_a, outs = [], []
        for hd in range(2):
            m8 = jnp.full((8, tq), NEG, F32)
            acc = jnp.zeros((MLA_V + 16, tq), F32)
            for c in range(n_chunks):
                m8 = pass_a_chunk(a_off, hd, c, sa_ref, m8)
                if t_b is not None:
                    acc = pass_b_chunk(hd, c, sb_ref, m_b[hd], acc)
            m_a.append(jnp.max(m8, axis=0, keepdims=True))
            if t_b is not None:
                outs.append(acc[:MLA_V] * (1.0 / acc[MLA_V:MLA_V + 1]))
        if t_b is not None:
            b_off = pl.multiple_of(t_b * tq, tq)
            o_ref[0, pl.ds(b_off, tq), :] = jnp.concatenate(outs, axis=0).T
        return tuple(m_a)

    def body(j, m_even):
        m_odd = stage(2 * j + 1, s1_ref, 2 * j, s0_ref, m_even)
        return stage(jnp.minimum(2 * j + 2, n_tiles - 1), s0_ref, 2 * j + 1, s1_ref, m_odd)

    m0 = stage(0, s0_ref, None, None, None)
    lax.fori_loop(0, n_tiles // 2, body, m0)


def _mla(qm, km, vt, tq, tk):
    b, h, s, _ = qm.shape
    return pl.pallas_call(
        functools.partial(_mla_kernel, tq=tq, tk=tk),
        grid=(b, h // 2),
        in_specs=[pl.BlockSpec((1, 2, s, LANES), lambda i, p: (i, p, 0, 0)),
                  pl.BlockSpec((1, 2, s, LANES), lambda i, p: (i, p, 0, 0)),
                  pl.BlockSpec((1, 2 * MLA_V, s), lambda i, p: (i, p, 0))],
        out_specs=pl.BlockSpec((1, s, LANES), lambda i, p: (i, 0, p)),
        out_shape=jax.ShapeDtypeStruct((b, s, MLA_WIDTH), F32),
        scratch_shapes=[pltpu.VMEM((2, s, tq), F32), pltpu.VMEM((2, s, tq), F32)],
        compiler_params=_params(("arbitrary", "arbitrary")),
        name="mla_attn",
    )(qm, km, vt)


def _out_kernel(ma_ref, ob_ref, x_ref, mod_ref, gb_ref, wo_ref, g2_ref, wr_ref, br_ref, tri_ref,
                x1_ref, h2_ref, rt_ref, rtm_ref, cnt_ref, carry_ref):
    first = (pl.program_id(0) == 0) & (pl.program_id(1) == 0)

    @pl.when(first)
    def _():
        carry_ref[...] = jnp.zeros_like(carry_ref)

    ob = ob_ref[0]
    mb = (ob * lax.rsqrt(jnp.mean(ob * ob, axis=-1, keepdims=True) + EPS) * gb_ref[...]).astype(BF16)
    y = _dot(ma_ref[0], wo_ref[0:SWA_WIDTH, :]) + _dot(mb, wo_ref[SWA_WIDTH:, :])
    x1 = x_ref[0] + mod_ref[0, 2:3, :] * y
    x1_ref[0] = x1
    h = x1 * lax.rsqrt(jnp.mean(x1 * x1, axis=-1, keepdims=True) + EPS) * g2_ref[...]
    h = h * (1.0 + mod_ref[0, 4:5, :]) + mod_ref[0, 3:4, :]
    h2_ref[0] = h
    h_hi, h_lo = _split_bf16(h)

    tm = h.shape[0]
    l_hi = _dot_nt(wr_ref[...], h_hi)
    l_lo = _dot_nt(wr_ref[0:ROUTE_ROWS, :], h_lo)
    logit = l_hi[0:ROUTE_ROWS] + l_hi[ROUTE_ROWS:] + l_lo + br_ref[...]
    row8 = lax.broadcasted_iota(jnp.int32, (8, tm), 0)
    gl = jnp.where(row8 < N_GROUPS, logit[0:8], NEG)
    gmax = jnp.max(gl, axis=0, keepdims=True)
    gsum = jnp.sum(jnp.exp(gl - gmax), axis=0, keepdims=True)
    g_idx = jnp.min(jnp.where(gl == gmax, row8, 8), axis=0, keepdims=True)
    g_w = 1.0 / gsum
    ec = logit[8:16]
    for g in range(1, N_GROUPS):
        ec = jnp.where(g_idx == g, logit[8 + 8 * g:16 + 8 * g], ec)
    m1 = jnp.max(ec, axis=0, keepdims=True)
    i1 = jnp.min(jnp.where(ec == m1, row8, 8), axis=0, keepdims=True)
    ec2 = jnp.where(row8 == i1, NEG, ec)
    m2 = jnp.max(ec2, axis=0, keepdims=True)
    i2 = jnp.min(jnp.where(ec2 == m2, row8, 8), axis=0, keepdims=True)
    t = jnp.exp(m2 - m1)
    w1 = g_w / (1.0 + t)
    w2 = g_w * t / (1.0 + t)
    e1 = g_idx * EXPERTS_PER_GROUP + i1
    e2 = g_idx * EXPERTS_PER_GROUP + i2

    rows = lax.broadcasted_iota(jnp.int32, (N_EXPERTS, tm), 0)
    oh1 = (rows == e1).astype(F32)
    oh2 = (rows == e2).astype(F32)
    both = oh1 + oh2
    incl = _dot(both.astype(BF16), tri_ref[...])
    before = carry_ref[:, 0:1] + incl - both
    r1 = jnp.sum(oh1 * before, axis=0, keepdims=True)
    r2 = jnp.sum(oh2 * before, axis=0, keepdims=True)
    total = carry_ref[:, 0:1] + incl[:, tm - 1:tm]
    carry_ref[...] = jnp.broadcast_to(total, carry_ref.shape)
    cnt_ref[...] = jnp.broadcast_to(total, cnt_ref.shape)

    route = jnp.concatenate([e1.astype(F32), e2.astype(F32), w1, w2, r1, r2, jnp.zeros((2, tm), F32)], axis=0)
    rt_ref[...] = route
    rtm_ref[...] = jnp.concatenate([route, jnp.zeros((LANES - 8, tm), F32)], axis=0).T


def _outproj(mixa, ob, x, mod3, g_mla, w_out, g2, wr, br, tm):
    b, s, _ = x.shape
    nt = s // tm
    tri = jnp.triu(jnp.ones((tm, tm), F32)).astype(BF16)
    full = lambda shp: pl.BlockSpec(shp, lambda i, j: (0,) * len(shp))
    tok = lambda w: pl.BlockSpec((1, tm, w), lambda i, j: (i, j, 0))
    return pl.pallas_call(
        _out_kernel,
        grid=(b, nt),
        in_specs=[tok(SWA_WIDTH), tok(MLA_WIDTH), tok(D_MODEL),
                  pl.BlockSpec((1, N_MOD, D_MODEL), lambda i, j: (i, 0, 0)),
                  full((1, MLA_WIDTH)), full((D_MODEL, D_MODEL)), full((1, D_MODEL)),
                  full((2 * ROUTE_ROWS, D_MODEL)), full((ROUTE_ROWS, 1)), full((tm, tm))],
        out_specs=[tok(D_MODEL), tok(D_MODEL),
                   pl.BlockSpec((8, tm), lambda i, j: (0, i * nt + j)),
                   pl.BlockSpec((tm, LANES), lambda i, j: (i * nt + j, 0)),
                   full((N_EXPERTS, LANES))],
        out_shape=[jax.ShapeDtypeStruct((b, s, D_MODEL), F32),
                   jax.ShapeDtypeStruct((b, s, D_MODEL), F32),
                   jax.ShapeDtypeStruct((8, b * s), F32),
                   jax.ShapeDtypeStruct((b * s, LANES), F32),
                   jax.ShapeDtypeStruct((N_EXPERTS, LANES), F32)],
        scratch_shapes=[pltpu.VMEM((N_EXPERTS, LANES), F32)],
        compiler_params=_params(("arbitrary", "arbitrary")),
        name="outproj_router",
    )(mixa, ob, x, mod3, g_mla, w_out, g2, wr, br, tri)


def _row_copy(src_ref, src_row, dst_ref, dst_row, sem):
    return pltpu.make_async_copy(src_ref.at[pl.ds(src_row, 1)], dst_ref.at[pl.ds(dst_row, 1)], sem)


def _dispatch_kernel(slot_ref, zst_ref, h_ref, xs_ref, zbuf, sem, zsem, *, tmd):
    i = pl.program_id(0)

    @pl.when(i == 0)
    def _():
        zbuf[...] = jnp.zeros_like(zbuf)
        for phase in ("start", "wait"):
            for e in range(N_EXPERTS):
                z = zst_ref[e]

                @pl.when(z >= 0)
                def _():
                    zrow = pl.multiple_of(z, EXPERT_TILE)
                    cp = pltpu.make_async_copy(zbuf, xs_ref.at[pl.ds(zrow, EXPERT_TILE)], zsem)
                    cp.start() if phase == "start" else cp.wait()

    def issue(r, carry):
        for k in range(2):
            _row_copy(h_ref, r, xs_ref, slot_ref[0, 0, 2 * r + k], sem).start(priority=k)
        return carry

    def drain(r, carry):
        _row_copy(h_ref, 0, xs_ref, 0, sem).wait()
        return carry

    lax.fori_loop(0, tmd, issue, 0, unroll=4)
    lax.fori_loop(0, 2 * tmd, drain, 0, unroll=8)


def _dispatch(slots3, zstart, h2, n_slots, tmd):
    t = h2.shape[0]
    return pl.pallas_call(
        functools.partial(_dispatch_kernel, tmd=tmd),
        grid=(t // tmd,),
        in_specs=[pl.BlockSpec((1, 1, 2 * tmd), lambda i: (i, 0, 0), memory_space=pltpu.SMEM),
                  pl.BlockSpec(memory_space=pltpu.SMEM),
                  pl.BlockSpec((tmd, D_MODEL), lambda i: (i, 0))],
        out_specs=pl.BlockSpec(memory_space=pl.ANY),
        out_shape=jax.ShapeDtypeStruct((n_slots, D_MODEL), F32),
        scratch_shapes=[pltpu.VMEM((EXPERT_TILE, D_MODEL), F32),
                        pltpu.SemaphoreType.DMA(()), pltpu.SemaphoreType.DMA(())],
        compiler_params=_params(("arbitrary",)),
        name="moe_dispatch",
    )(slots3, zstart, h2)


def _expert_kernel(te_ref, nt_ref, xs_ref, wg_ref, wu_ref, wd_ref, y_ref, wgu_buf, wd_buf):
    i = pl.program_id(0)

    @pl.when(i < nt_ref[0])
    def _():
        changed = (i == 0) | (te_ref[i] != te_ref[jnp.maximum(i - 1, 0)])

        @pl.when(changed)
        def _():
            wgu_buf[:, 0:D_EXPERT] = wg_ref[0].astype(BF16)
            wgu_buf[:, D_EXPERT:] = wu_ref[0].astype(BF16)
            wd_buf[...] = wd_ref[0].astype(BF16)

        gu = _dot(xs_ref[...].astype(BF16), wgu_buf[...])
        g = gu[:, 0:D_EXPERT]
        hid = (g * (1.0 / (1.0 + jnp.exp(-g))) * gu[:, D_EXPERT:]).astype(BF16)
        y_ref[...] = _dot(hid, wd_buf[...])


def _experts(tile_expert, n_tiles, xs, wg, wu, wd):
    max_tiles = xs.shape[0] // EXPERT_TILE
    row = lambda i, te, nt: (jnp.minimum(i, nt[0] - 1), 0)
    wsel = lambda i, te, nt: (te[i], 0, 0)
    return pl.pallas_call(
        _expert_kernel,
        grid_spec=pltpu.PrefetchScalarGridSpec(
            num_scalar_prefetch=2,
            grid=(max_tiles,),
            in_specs=[pl.BlockSpec((EXPERT_TILE, D_MODEL), row),
                      pl.BlockSpec((1, D_MODEL, D_EXPERT), wsel),
                      pl.BlockSpec((1, D_MODEL, D_EXPERT), wsel),
                      pl.BlockSpec((1, D_EXPERT, D_MODEL), wsel)],
            out_specs=pl.BlockSpec((EXPERT_TILE, D_MODEL), row),
            scratch_shapes=[pltpu.VMEM((D_MODEL, 2 * D_EXPERT), BF16),
                            pltpu.VMEM((D_EXPERT, D_MODEL), BF16)]),
        out_shape=jax.ShapeDtypeStruct(xs.shape, F32),
        compiler_params=_params(("arbitrary",)),
        name="moe_experts",
    )(tile_expert, n_tiles, xs, wg, wu, wd)


def _combine_kernel(slot_ref, slot_next_ref, y_ref, rt_ref, x1_ref, mod_ref, gf_ref, o_ref, ybuf, sems, *, tmc):
    i = pl.program_id(0)
    n = pl.num_programs(0)

    def fetch(slots, half):
        def issue(r, carry):
            for k in range(2):
                _row_copy(y_ref, slots[0, 0, 2 * r + k], ybuf.at[half, k], r, sems.at[half]).start(priority=k)
            return carry

        lax.fori_loop(0, tmc, issue, 0, unroll=4)

    @pl.when(i == 0)
    def _():
        fetch(slot_ref, 0)

    @pl.when(i + 1 < n)
    def _():
        fetch(slot_next_ref, (i + 1) % 2)

    cur = i % 2

    def drain(r, carry):
        _row_copy(y_ref, 0, ybuf.at[cur, 0], 0, sems.at[cur]).wait()
        return carry

    lax.fori_loop(0, 2 * tmc, drain, 0, unroll=8)
    moe = rt_ref[:, 2:3] * ybuf[cur, 0] + rt_ref[:, 3:4] * ybuf[cur, 1]
    v = x1_ref[...] + mod_ref[0, 5:6, :] * moe
    o_ref[...] = v * lax.rsqrt(jnp.mean(v * v, axis=-1, keepdims=True) + EPS) * gf_ref[...]


def _combine(slots3, y, rtm, x1, mod3, gf, seq, tmc):
    t = x1.shape[0]
    per_b = seq // tmc
    n_steps = t // tmc
    tok = lambda w: pl.BlockSpec((tmc, w), lambda i: (i, 0))
    return pl.pallas_call(
        functools.partial(_combine_kernel, tmc=tmc),
        grid=(n_steps,),
        in_specs=[pl.BlockSpec((1, 1, 2 * tmc), lambda i: (i, 0, 0), memory_space=pltpu.SMEM),
                  pl.BlockSpec((1, 1, 2 * tmc), lambda i: (jnp.minimum(i + 1, n_steps - 1), 0, 0),
                               memory_space=pltpu.SMEM),
                  pl.BlockSpec(memory_space=pl.ANY),
                  tok(LANES), tok(D_MODEL),
                  pl.BlockSpec((1, N_MOD, D_MODEL), lambda i: (i // per_b, 0, 0)),
                  pl.BlockSpec((1, D_MODEL), lambda i: (0, 0))],
        out_specs=tok(D_MODEL),
        out_shape=jax.ShapeDtypeStruct((t, D_MODEL), F32),
        scratch_shapes=[pltpu.VMEM((2, 2, tmc, D_MODEL), F32), pltpu.SemaphoreType.DMA((2,))],
        compiler_params=_params(("arbitrary",)),
        name="moe_combine_final",
    )(slots3, slots3, y, rtm, x1, mod3, gf)


def _plan_slots(route_t, counts, n_tokens):
    e = route_t[0:2].astype(jnp.int32)
    rank = route_t[4:6].astype(jnp.int32)
    cnt = counts[:, 0].astype(jnp.int32)
    tiles = (cnt + EXPERT_TILE - 1) // EXPERT_TILE
    ids = jnp.arange(N_EXPERTS, dtype=jnp.int32)
    tile_end = jnp.sum(jnp.where(ids[:, None] <= ids[None, :], tiles[:, None], 0), axis=0)
    base = (tile_end - tiles) * EXPERT_TILE
    base_of = jnp.sum(jnp.where(e[:, :, None] == ids[None, None, :], base[None, None, :], 0), axis=-1)
    slots = (base_of + rank).T
    max_tiles = (2 * n_tokens + N_EXPERTS * (EXPERT_TILE - 1)) // EXPERT_TILE
    tile_ids = jnp.arange(max_tiles, dtype=jnp.int32)
    tile_expert = jnp.minimum(jnp.sum((tile_ids[:, None] >= tile_end[None, :]).astype(jnp.int32), axis=1),
                              N_EXPERTS - 1)
    zstart = jnp.where(tiles > 0, (tile_end - 1) * EXPERT_TILE, -1).astype(jnp.int32)
    return slots, tile_expert, tile_end[-1:].astype(jnp.int32), zstart, max_tiles * EXPERT_TILE


def _rope_tables(seq, dim):
    inv = 1.0 / (ROPE_THETA ** (jnp.arange(0, dim, 2, dtype=F32) / dim))
    ang = jnp.arange(seq, dtype=F32)[:, None] * inv[None, :]
    return jnp.cos(ang), jnp.sin(ang)


def _tables(seq):
    ca, sa = _rope_tables(seq, SWA_HEAD_DIM)
    cb, sb = _rope_tables(seq, MLA_ROPE)
    z32 = jnp.zeros_like(sa)
    z16 = jnp.zeros_like(sb)
    z64 = jnp.zeros((seq, 64), F32)
    one64 = jnp.ones((seq, 64), F32)
    cat = lambda parts: jnp.concatenate(parts, axis=1)
    scale = float((MLA_NOPE + MLA_ROPE) ** -0.5 * np.log2(np.e))
    swa = [cat([ca] * 4), cat([z32, sa] * 2), cat([-sa, z32] * 2)]
    mq = [cat([one64, cb, cb, z32]) * scale, cat([z64, z16, sb, z32]) * scale, cat([z64, -sb, z16, z32]) * scale]
    mk = [cat([z64, cb, cb, z32]), cat([z64, z16, sb, z32]), cat([z64, -sb, z16, z32])]
    return jnp.stack(swa + mq + mk, axis=0)


def _layout_weights(w_in, w_uq, w_ukv, g_q):
    qa, ka, va, cq, ckv, kr = jnp.split(w_in, np.cumsum([512, 128, 128, 192, 128, 32])[:-1].tolist(), axis=1)
    zc = lambda n: jnp.zeros((D_MODEL, n), F32)
    w_in_l = jnp.concatenate([qa * (SWA_HEAD_DIM ** -0.5), ka, va, ckv, cq, zc(64), zc(64), kr, zc(32)], axis=1)
    uq = w_uq.reshape(MLA_Q_RANK, MLA_HEADS, MLA_NOPE + MLA_ROPE)
    uq = jnp.pad(uq, ((0, MLA_Q_RANK_PAD - MLA_Q_RANK), (0, 0), (0, LANES - MLA_NOPE - MLA_ROPE)))
    ukv = w_ukv.reshape(MLA_KV_RANK, MLA_HEADS, MLA_NOPE + MLA_V)
    wk = jnp.pad(ukv[:, :, :MLA_NOPE], ((0, 0), (0, 0), (0, LANES - MLA_NOPE)))
    wvt = ukv[:, :, MLA_NOPE:].reshape(MLA_KV_RANK, MLA_WIDTH).T
    gq = jnp.pad(g_q, (0, MLA_Q_RANK_PAD - MLA_Q_RANK))[None, :]
    return (w_in_l.astype(BF16), uq.reshape(MLA_Q_RANK_PAD, MLA_HEADS * LANES).astype(BF16),
            wk.reshape(MLA_KV_RANK, MLA_HEADS * LANES).astype(BF16), wvt.astype(BF16), gq)


def _layout_router(w_rg, b_rg, w_re, b_re):
    w = jnp.zeros((ROUTE_ROWS, D_MODEL), F32)
    w = w.at[0:N_GROUPS].set(w_rg.T).at[8:8 + N_EXPERTS].set(w_re.T)
    bias = jnp.zeros((ROUTE_ROWS,), F32).at[0:N_GROUPS].set(b_rg).at[8:8 + N_EXPERTS].set(b_re)
    hi = w.astype(BF16)
    lo = (w - hi.astype(F32)).astype(BF16)
    return jnp.concatenate([hi, lo], axis=0), bias[:, None]


def kernel(x, c, w_ada, b_ada, g_norm1, w_in, g_q_lora, w_uq, g_kv_lora, w_ukv, sink, g_out_swa, g_out_mla, w_out, g_norm2, w_router_group, b_router_group, w_router_expert, b_router_expert, w_exp_gate, w_exp_up, w_exp_down, g_final):
    b, s, d = x.shape
    assert d == D_MODEL and s % 512 == 0 and b <= 8
    assert w_ada.shape[0] == 1, "single layer"

    c8 = jnp.pad(c, ((0, 8 - b), (0, 0)))
    mod3 = _ada(c8, w_ada[0], b_ada[0][None, :]).reshape(8, N_MOD, D_MODEL)

    w_in_l, w_uq_l, w_k_l, w_vt_l, gq = _layout_weights(w_in[0], w_uq[0], w_ukv[0], g_q_lora[0])
    qa, ka, va, qm, km, vt = _proj(x, mod3, g_norm1[0][None, :], _tables(s), w_in_l, gq,
                                   g_kv_lora[0][None, :], w_uq_l, w_k_l, w_vt_l, tm=256)

    mixa = _swa(sink[0][None, :], qa, ka, va, g_out_swa[0][None, :])
    ob = _mla(qm, km, vt, tq=256, tk=256)

    wr, br = _layout_router(w_router_group[0], b_router_group[0], w_router_expert[0], b_router_expert[0])
    x1, h2, route_t, rtm, counts = _outproj(mixa, ob, x, mod3, g_out_mla[0][None, :], w_out[0].astype(BF16),
                                            g_norm2[0][None, :], wr, br, tm=256)

    t = b * s
    slots, tile_expert, n_tiles, zstart, n_slots = _plan_slots(route_t, counts, t)
    tmd, tmc = 512, 256
    xs = _dispatch(slots.reshape(t // tmd, 1, 2 * tmd), zstart, h2.reshape(t, d), n_slots, tmd)
    y = _experts(tile_expert, n_tiles, xs, w_exp_gate[0], w_exp_up[0], w_exp_down[0])
    out = _combine(slots.reshape(t // tmc, 1, 2 * tmc), y, rtm, x1.reshape(t, d), mod3, g_final[None, :], s, tmc)
    return out.reshape(b, s, d)
```

```python
import functools

import jax
import jax.numpy as jnp
import numpy as np
from jax import lax
from jax.experimental import pallas as pl
from jax.experimental.pallas import tpu as pltpu

F32 = jnp.float32
BF16 = jnp.bfloat16

D_MODEL = 1024
EPS = 1e-6
ROPE_THETA = 10000.0
NEG = -1e30
LANES = 128

BLOCK = 128
SWA_HEADS = 8
SWA_KV_HEADS = 2
SWA_HEAD_DIM = 64
SWA_WIDTH = SWA_HEADS * SWA_HEAD_DIM

MLA_HEADS = 8
MLA_NOPE = 64
MLA_ROPE = 32
MLA_V = 64
MLA_Q_RANK = 192
MLA_Q_RANK_PAD = 256
MLA_KV_RANK = 128
MLA_WIDTH = MLA_HEADS * MLA_V

N_GROUPS = 4
EXPERTS_PER_GROUP = 8
N_EXPERTS = 32
D_EXPERT = 256
N_MOD = 6

COL_QA = 0
COL_KA = 512
COL_VA = 640
COL_CKV = 768
COL_CQ = 896
COL_KR = 1152
IN_COLS = 1280

EXPERT_TILE = 256
ROUTE_ROWS = 48

VMEM_LIMIT = 56 * 1024 * 1024

NT_DIMS = (((1,), (1,)), ((), ()))


def _dot(a, b):
    return jnp.dot(a, b, preferred_element_type=F32)


def _dot_nt(a, b):
    return lax.dot_general(a, b, NT_DIMS, preferred_element_type=F32)


def _split_bf16(a):
    hi = a.astype(BF16)
    lo = (a - hi.astype(F32)).astype(BF16)
    return hi, lo


def _params(sem):
    return pltpu.CompilerParams(dimension_semantics=sem, vmem_limit_bytes=VMEM_LIMIT)


def _ada_kernel(c_ref, w_ref, b_ref, o_ref):
    c = c_ref[...]
    a = c * (1.0 / (1.0 + jnp.exp(-c)))
    a_hi, a_lo = _split_bf16(a)
    w_hi, w_lo = _split_bf16(w_ref[...])
    o_ref[...] = _dot(a_hi, w_hi) + _dot(a_lo, w_hi) + _dot(a_hi, w_lo) + b_ref[...]


def _ada(c8, w_ada, b_ada):
    n = w_ada.shape[1]
    tn = 1024
    return pl.pallas_call(
        _ada_kernel,
        grid=(n // tn,),
        in_specs=[pl.BlockSpec((8, D_MODEL), lambda j: (0, 0)),
                  pl.BlockSpec((D_MODEL, tn), lambda j: (0, j)),
                  pl.BlockSpec((1, tn), lambda j: (0, j))],
        out_specs=pl.BlockSpec((8, tn), lambda j: (0, j)),
        out_shape=jax.ShapeDtypeStruct((8, n), F32),
        compiler_params=_params(("arbitrary",)),
        name="ada_mod",
    )(c8, w_ada, b_ada)


def _rope_slab(x, c, s1, s2, sh):
    return x * c + pltpu.roll(x, sh, 1) * s1 + pltpu.roll(x, LANES - sh, 1) * s2


def _proj_kernel(x_ref, mod_ref, g1_ref, tab_ref, win_ref, gq_ref, gkv_ref, wuq_ref, wk_ref, wvt_ref,
                 qa_ref, ka_ref, va_ref, qm_ref, km_ref, vt_ref):
    x = x_ref[0]
    ms = jnp.mean(x * x, axis=-1, keepdims=True)
    h = x * lax.rsqrt(ms + EPS) * g1_ref[...]
    h = h * (1.0 + mod_ref[0, 1:2, :]) + mod_ref[0, 0:1, :]
    proj = _dot(h.astype(BF16), win_ref[...])

    ca, sa1, sa2 = tab_ref[0], tab_ref[1], tab_ref[2]
    for s in range(SWA_WIDTH // LANES):
        q = proj[:, COL_QA + s * LANES:COL_QA + (s + 1) * LANES]
        qa_ref[0, :, s * LANES:(s + 1) * LANES] = _rope_slab(q, ca, sa1, sa2, 32).astype(BF16)
    ka_ref[0] = _rope_slab(proj[:, COL_KA:COL_KA + LANES], ca, sa1, sa2, 32).astype(BF16)
    va_ref[0] = proj[:, COL_VA:COL_VA + LANES].astype(BF16)

    ckv = proj[:, COL_CKV:COL_CKV + MLA_KV_RANK]
    ckvn = (ckv * lax.rsqrt(jnp.mean(ckv * ckv, axis=-1, keepdims=True) + EPS) * gkv_ref[...]).astype(BF16)
    cq = proj[:, COL_CQ:COL_CQ + MLA_Q_RANK_PAD]
    cq_ms = jnp.sum(cq * cq, axis=-1, keepdims=True) * (1.0 / MLA_Q_RANK)
    cqn = (cq * lax.rsqrt(cq_ms + EPS) * gq_ref[...]).astype(BF16)

    kr = _rope_slab(proj[:, COL_KR:COL_KR + LANES], tab_ref[6], tab_ref[7], tab_ref[8], 16)
    qm = _dot(cqn, wuq_ref[...])
    kn = _dot(ckvn, wk_ref[...])
    cq_t, sq1, sq2 = tab_ref[3], tab_ref[4], tab_ref[5]
    for hd in range(MLA_HEADS):
        sl = slice(hd * LANES, (hd + 1) * LANES)
        qm_ref[0, hd] = _rope_slab(qm[:, sl], cq_t, sq1, sq2, 16).astype(BF16)
        km_ref[0, hd] = (kn[:, sl] + kr).astype(BF16)
    vt_ref[0] = _dot_nt(wvt_ref[...], ckvn).astype(BF16)


def _proj(x, mod3, g1, tabs, w_in, gq, gkv, w_uq, w_k, w_vt, tm):
    b, s, _ = x.shape
    full = lambda shp: pl.BlockSpec(shp, lambda i, j: (0,) * len(shp))
    return pl.pallas_call(
        _proj_kernel,
        grid=(s // tm, b),
        in_specs=[pl.BlockSpec((1, tm, D_MODEL), lambda i, j: (j, i, 0)),
                  pl.BlockSpec((1, N_MOD, D_MODEL), lambda i, j: (j, 0, 0)),
                  full((1, D_MODEL)),
                  pl.BlockSpec((9, tm, LANES), lambda i, j: (0, i, 0)),
                  full((D_MODEL, IN_COLS)),
                  full((1, MLA_Q_RANK_PAD)),
                  full((1, MLA_KV_RANK)),
                  full((MLA_Q_RANK_PAD, MLA_HEADS * LANES)),
                  full((MLA_KV_RANK, MLA_HEADS * LANES)),
                  full((MLA_WIDTH, MLA_KV_RANK))],
        out_specs=[pl.BlockSpec((1, tm, SWA_WIDTH), lambda i, j: (j, i, 0)),
                   pl.BlockSpec((1, tm, LANES), lambda i, j: (j, i, 0)),
                   pl.BlockSpec((1, tm, LANES), lambda i, j: (j, i, 0)),
                   pl.BlockSpec((1, MLA_HEADS, tm, LANES), lambda i, j: (j, 0, i, 0)),
                   pl.BlockSpec((1, MLA_HEADS, tm, LANES), lambda i, j: (j, 0, i, 0)),
                   pl.BlockSpec((1, MLA_WIDTH, tm), lambda i, j: (j, 0, i))],
        out_shape=[jax.ShapeDtypeStruct((b, s, SWA_WIDTH), BF16),
                   jax.ShapeDtypeStruct((b, s, LANES), BF16),
                   jax.ShapeDtypeStruct((b, s, LANES), BF16),
                   jax.ShapeDtypeStruct((b, MLA_HEADS, s, LANES), BF16),
                   jax.ShapeDtypeStruct((b, MLA_HEADS, s, LANES), BF16),
                   jax.ShapeDtypeStruct((b, MLA_WIDTH, s), BF16)],
        compiler_params=_params(("arbitrary", "arbitrary")),
        name="norm1_proj",
    )(x, mod3, g1, tabs, w_in, gq, gkv, w_uq, w_k, w_vt)


def _swa_kernel(sink_ref, q_ref, kp_ref, kc_ref, kn_ref, vp_ref, vc_ref, vn_ref, g_ref, o_ref, *, nb):
    n = pl.program_id(1)
    k = jnp.concatenate([kp_ref[0], kc_ref[0], kn_ref[0]], axis=0)
    v = jnp.concatenate([vp_ref[0], vc_ref[0], vn_ref[0]], axis=0)
    k_sw = pltpu.roll(k, SWA_HEAD_DIM, 1)
    v_sw = pltpu.roll(v, SWA_HEAD_DIM, 1)
    lane = lax.broadcasted_iota(jnp.int32, k.shape, 1)
    lo = lane < SWA_HEAD_DIM
    zero = jnp.zeros_like(k)
    k_var = ((jnp.where(lo, k, zero), jnp.where(lo, zero, k_sw)),
             (jnp.where(lo, k_sw, zero), jnp.where(lo, zero, k)))
    v_var = ((jnp.where(lo, v, zero), jnp.where(lo, zero, v_sw)),
             (jnp.where(lo, v_sw, zero), jnp.where(lo, zero, v)))

    r = lax.broadcasted_iota(jnp.int32, (BLOCK, 3 * BLOCK), 0)
    j = lax.broadcasted_iota(jnp.int32, (BLOCK, 3 * BLOCK), 1)
    mask = (j >= r) & (j <= r + 2 * BLOCK)
    mask = mask & ((j >= BLOCK) | (n > 0)) & ((j < 2 * BLOCK) | (n < nb - 1))

    group = SWA_HEADS // SWA_KV_HEADS

    def scores(head):
        s, half = divmod(head, 2)
        q = q_ref[0, :, s * LANES:(s + 1) * LANES]
        return _dot_nt(q, k_var[head // group][half])

    ahead = 8
    pending = [scores(h) for h in range(ahead)]
    slabs = []
    ssq = jnp.zeros((BLOCK, 1), F32)
    for head in range(SWA_HEADS):
        if head + ahead < SWA_HEADS:
            pending.append(scores(head + ahead))
        sc = jnp.where(mask, pending[head], NEG)
        snk = sink_ref[0, head]
        m = jnp.maximum(jnp.max(sc, axis=-1, keepdims=True), snk)
        p = jnp.exp(sc - m)
        denom = jnp.sum(p, axis=-1, keepdims=True) + jnp.exp(snk - m)
        o_head = _dot(p.astype(BF16), v_var[head // group][head % 2]) * (1.0 / denom)
        if head % 2 == 0:
            o_slab = o_head
        else:
            o_slab = o_slab + o_head
            slabs.append(o_slab)
            ssq = ssq + jnp.sum(o_slab * o_slab, axis=-1, keepdims=True)
    scale = lax.rsqrt(ssq * (1.0 / SWA_WIDTH) + EPS)
    for s, o_slab in enumerate(slabs):
        sl = slice(s * LANES, (s + 1) * LANES)
        o_ref[0, :, sl] = (o_slab * scale * g_ref[:, sl]).astype(BF16)


def _swa(sink, qa, ka, va, g_out):
    b, s, _ = qa.shape
    nb = s // BLOCK
    kv_spec = lambda f: pl.BlockSpec((1, BLOCK, LANES), f)
    prev = lambda i, n: (i, jnp.maximum(n - 1, 0), 0)
    cur = lambda i, n: (i, n, 0)
    nxt = lambda i, n: (i, jnp.minimum(n + 1, nb - 1), 0)
    return pl.pallas_call(
        functools.partial(_swa_kernel, nb=nb),
        grid=(b, nb),
        in_specs=[pl.BlockSpec(memory_space=pltpu.SMEM),
                  pl.BlockSpec((1, BLOCK, SWA_WIDTH), cur),
                  kv_spec(prev), kv_spec(cur), kv_spec(nxt),
                  kv_spec(prev), kv_spec(cur), kv_spec(nxt),
                  pl.BlockSpec((1, SWA_WIDTH), lambda i, n: (0, 0))],
        out_specs=pl.BlockSpec((1, BLOCK, SWA_WIDTH), cur),
        out_shape=jax.ShapeDtypeStruct((b, s, SWA_WIDTH), BF16),
        compiler_params=_params(("arbitrary", "arbitrary")),
        name="swa_attn",
    )(sink, qa, ka, ka, ka, va, va, va, g_out)


def _mla_kernel(q_ref, k_ref, vt_ref, o_ref, s0_ref, s1_ref, *, tq, tk):
    s_len = k_ref.shape[2]
    n_tiles = s_len // tq
    n_chunks = s_len // tk
    ones = jnp.ones((16, tk), BF16)

    def pass_a_chunk(t_off, hd, c, s_ref, m8):
        q = q_ref[0, hd, pl.ds(t_off, tq), :]
        st = _dot_nt(k_ref[0, hd, c * tk:(c + 1) * tk, :], q)
        s_ref[hd, c * tk:(c + 1) * tk, :] = st
        return jnp.maximum(m8, jnp.max(st.reshape(tk // 8, 8, tq), axis=0))

    def pass_b_chunk(hd, c, s_ref, m, acc):
        p = jnp.exp2(s_ref[hd, c * tk:(c + 1) * tk, :] - m).astype(BF16)
        v = vt_ref[0, hd * MLA_V:(hd + 1) * MLA_V, c * tk:(c + 1) * tk]
        vext = jnp.concatenate([v, ones], axis=0)
        return acc + _dot(vext, p)

    def stage(t_a, sa_ref, t_b, sb_ref, m_b):
        a_off = pl.multiple_of(t_a * tq, tq)
        m_a, outs = [], []
        for hd in range(2):
            m8 = jnp.full((8, tq), NEG, F32)
            acc = jnp.zeros((MLA_V + 16, tq), F32)
            for c in range(n_chunks):
                m8 = pass_a_chunk(a_off, hd, c, sa_ref, m8)
                if t_b is not None:
                    acc = pass_b_chunk(hd, c, sb_ref, m_b[hd], acc)
            m_a.append(jnp.max(m8, axis=0, keepdims=True))
            if t_b is not None:
                outs.append(acc[:MLA_V] * (1.0 / acc[MLA_V:MLA_V + 1]))
        if t_b is not None:
            b_off = pl.multiple_of(t_b * tq, tq)
            o_ref[0, pl.ds(b_off, tq), :] = jnp.concatenate(outs, axis=0).T
        return tuple(m_a)

    def body(j, m_even):
        m_odd = stage(2 * j + 1, s1_ref, 2 * j, s0_ref, m_even)
        return stage(jnp.minimum(2 * j + 2, n_tiles - 1), s0_ref, 2 * j + 1, s1_ref, m_odd)

    m0 = stage(0, s0_ref, None, None, None)
    lax.fori_loop(0, n_tiles // 2, body, m0)


def _mla(qm, km, vt, tq, tk):
    b, h, s, _ = qm.shape
    return pl.pallas_call(
        functools.partial(_mla_kernel, tq=tq, tk=tk),
        grid=(b, h // 2),
        in_specs=[pl.BlockSpec((1, 2, s, LANES), lambda i, p: (i, p, 0, 0)),
                  pl.BlockSpec((1, 2, s, LANES), lambda i, p: (i, p, 0, 0)),
                  pl.BlockSpec((1, 2 * MLA_V, s), lambda i, p: (i, p, 0))],
        out_specs=pl.BlockSpec((1, s, LANES), lambda i, p: (i, 0, p)),
        out_shape=jax.ShapeDtypeStruct((b, s, MLA_WIDTH), F32),
        scratch_shapes=[pltpu.VMEM((2, s, tq), F32), pltpu.VMEM((2, s, tq), F32)],
        compiler_params=_params(("arbitrary", "arbitrary")),
        name="mla_attn",
    )(qm, km, vt)


def _out_kernel(ma_ref, ob_ref, x_ref, mod_ref, gb_ref, wo_ref, g2_ref, wr_ref, br_ref, tri_ref,
                x1_ref, h2_ref, rt_ref, rtm_ref, cnt_ref, carry_ref):
    first = (pl.program_id(0) == 0) & (pl.program_id(1) == 0)

    @pl.when(first)
    def _():
        carry_ref[...] = jnp.zeros_like(carry_ref)

    ob = ob_ref[0]
    mb = (ob * lax.rsqrt(jnp.mean(ob * ob, axis=-1, keepdims=True) + EPS) * gb_ref[...]).astype(BF16)
    y = _dot(ma_ref[0], wo_ref[0:SWA_WIDTH, :]) + _dot(mb, wo_ref[SWA_WIDTH:, :])
    x1 = x_ref[0] + mod_ref[0, 2:3, :] * y
    x1_ref[0] = x1
    h = x1 * lax.rsqrt(jnp.mean(x1 * x1, axis=-1, keepdims=True) + EPS) * g2_ref[...]
    h = h * (1.0 + mod_ref[0, 4:5, :]) + mod_ref[0, 3:4, :]
    h2_ref[0] = h
    h_hi, h_lo = _split_bf16(h)

    tm = h.shape[0]
    l_hi = _dot_nt(wr_ref[...], h_hi)
    l_lo = _dot_nt(wr_ref[0:ROUTE_ROWS, :], h_lo)
    logit = l_hi[0:ROUTE_ROWS] + l_hi[ROUTE_ROWS:] + l_lo + br_ref[...]
    row8 = lax.broadcasted_iota(jnp.int32, (8, tm), 0)
    gl = jnp.where(row8 < N_GROUPS, logit[0:8], NEG)
    gmax = jnp.max(gl, axis=0, keepdims=True)
    gsum = jnp.sum(jnp.exp(gl - gmax), axis=0, keepdims=True)
    g_idx = jnp.min(jnp.where(gl == gmax, row8, 8), axis=0, keepdims=True)
    g_w = 1.0 / gsum
    ec = logit[8:16]
    for g in range(1, N_GROUPS):
        ec = jnp.where(g_idx == g, logit[8 + 8 * g:16 + 8 * g], ec)
    m1 = jnp.max(ec, axis=0, keepdims=True)
    i1 = jnp.min(jnp.where(ec == m1, row8, 8), axis=0, keepdims=True)
    ec2 = jnp.where(row8 == i1, NEG, ec)
    m2 = jnp.max(ec2, axis=0, keepdims=True)
    i2 = jnp.min(jnp.where(ec2 == m2, row8, 8), axis=0, keepdims=True)
    t = jnp.exp(m2 - m1)
    w1 = g_w / (1.0 + t)
    w2 = g_w * t / (1.0 + t)
    e1 = g_idx * EXPERTS_PER_GROUP + i1
    e2 = g_idx * EXPERTS_PER_GROUP + i2

    rows = lax.broadcasted_iota(jnp.int32, (N_EXPERTS, tm), 0)
    oh1 = (rows == e1).astype(F32)
    oh2 = (rows == e2).astype(F32)
    both = oh1 + oh2
    incl = _dot(both.astype(BF16), tri_ref[...])
    before = carry_ref[:, 0:1] + incl - both
    r1 = jnp.sum(oh1 * before, axis=0, keepdims=True)
    r2 = jnp.sum(oh2 * before, axis=0, keepdims=True)
    total = carry_ref[:, 0:1] + incl[:, tm - 1:tm]
    carry_ref[...] = jnp.broadcast_to(total, carry_ref.shape)
    cnt_ref[...] = jnp.broadcast_to(total, cnt_ref.shape)

    route = jnp.concatenate([e1.astype(F32), e2.astype(F32), w1, w2, r1, r2, jnp.zeros((2, tm), F32)], axis=0)
    rt_ref[...] = route
    rtm_ref[...] = jnp.concatenate([route, jnp.zeros((LANES - 8, tm), F32)], axis=0).T


def _outproj(mixa, ob, x, mod3, g_mla, w_out, g2, wr, br, tm):
    b, s, _ = x.shape
    nt = s // tm
    tri = jnp.triu(jnp.ones((tm, tm), F32)).astype(BF16)
    full = lambda shp: pl.BlockSpec(shp, lambda i, j: (0,) * len(shp))
    tok = lambda w: pl.BlockSpec((1, tm, w), lambda i, j: (i, j, 0))
    return pl.pallas_call(
        _out_kernel,
        grid=(b, nt),
        in_specs=[tok(SWA_WIDTH), tok(MLA_WIDTH), tok(D_MODEL),
                  pl.BlockSpec((1, N_MOD, D_MODEL), lambda i, j: (i, 0, 0)),
                  full((1, MLA_WIDTH)), full((D_MODEL, D_MODEL)), full((1, D_MODEL)),
                  full((2 * ROUTE_ROWS, D_MODEL)), full((ROUTE_ROWS, 1)), full((tm, tm))],
        out_specs=[tok(D_MODEL), tok(D_MODEL),
                   pl.BlockSpec((8, tm), lambda i, j: (0, i * nt + j)),
                   pl.BlockSpec((tm, LANES), lambda i, j: (i * nt + j, 0)),
                   full((N_EXPERTS, LANES))],
        out_shape=[jax.ShapeDtypeStruct((b, s, D_MODEL), F32),
                   jax.ShapeDtypeStruct((b, s, D_MODEL), F32),
                   jax.ShapeDtypeStruct((8, b * s), F32),
                   jax.ShapeDtypeStruct((b * s, LANES), F32),
                   jax.ShapeDtypeStruct((N_EXPERTS, LANES), F32)],
        scratch_shapes=[pltpu.VMEM((N_EXPERTS, LANES), F32)],
        compiler_params=_params(("arbitrary", "arbitrary")),
        name="outproj_router",
    )(mixa, ob, x, mod3, g_mla, w_out, g2, wr, br, tri)


def _row_copy(src_ref, src_row, dst_ref, dst_row, sem):
    return pltpu.make_async_copy(src_ref.at[pl.ds(src_row, 1)], dst_ref.at[pl.ds(dst_row, 1)], sem)


def _dispatch_kernel(slot_ref, zst_ref, h_ref, xs_ref, zbuf, sem, zsem, *, tmd):
    i = pl.program_id(0)

    @pl.when(i == 0)
    def _():
        zbuf[...] = jnp.zeros_like(zbuf)
        for phase in ("start", "wait"):
            for e in range(zst_ref.shape[0]):
                z = zst_ref[e]

                @pl.when(z >= 0)
                def _():
                    zrow = pl.multiple_of(z, EXPERT_TILE)
                    cp = pltpu.make_async_copy(zbuf, xs_ref.at[pl.ds(zrow, EXPERT_TILE)], zsem)
                    cp.start() if phase == "start" else cp.wait()

    def issue(r, carry):
        for k in range(2):
            _row_copy(h_ref, r, xs_ref, slot_ref[0, 0, 2 * r + k], sem).start(priority=k)
        return carry

    def drain(r, carry):
        _row_copy(h_ref, 0, xs_ref, 0, sem).wait()
        return carry

    lax.fori_loop(0, tmd, issue, 0, unroll=4)
    lax.fori_loop(0, 2 * tmd, drain, 0, unroll=8)


def _dispatch(slots3, zstart, h2, n_slots, tmd):
    t = h2.shape[0]
    return pl.pallas_call(
        functools.partial(_dispatch_kernel, tmd=tmd),
        grid=(t // tmd,),
        in_specs=[pl.BlockSpec((1, 1, 2 * tmd), lambda i: (i, 0, 0), memory_space=pltpu.SMEM),
                  pl.BlockSpec(memory_space=pltpu.SMEM),
                  pl.BlockSpec((tmd, D_MODEL), lambda i: (i, 0))],
        out_specs=pl.BlockSpec(memory_space=pl.ANY),
        out_shape=jax.ShapeDtypeStruct((n_slots, D_MODEL), F32),
        scratch_shapes=[pltpu.VMEM((EXPERT_TILE, D_MODEL), F32),
                        pltpu.SemaphoreType.DMA(()), pltpu.SemaphoreType.DMA(())],
        compiler_params=_params(("arbitrary",)),
        name="moe_dispatch",
    )(slots3, zstart, h2)


def _expert_kernel(te_ref, nt_ref, xs_ref, wg_ref, wu_ref, wd_ref, y_ref, wgu_buf, wd_buf):
    i = pl.program_id(0)

    @pl.when(i < nt_ref[0])
    def _():
        changed = (i == 0) | (te_ref[i] != te_ref[jnp.maximum(i - 1, 0)])

        @pl.when(changed)
        def _():
            wgu_buf[:, 0:D_EXPERT] = wg_ref[0].astype(BF16)
            wgu_buf[:, D_EXPERT:] = wu_ref[0].astype(BF16)
            wd_buf[...] = wd_ref[0].astype(BF16)

        gu = _dot(xs_ref[...].astype(BF16), wgu_buf[...])
        g = gu[:, 0:D_EXPERT]
        hid = (g * (1.0 / (1.0 + jnp.exp(-g))) * gu[:, D_EXPERT:]).astype(BF16)
        y_ref[...] = _dot(hid, wd_buf[...])

    @pl.when(i >= nt_ref[0])
    def _():
        y_ref[...] = jnp.zeros_like(y_ref)


def _experts(tile_expert, n_tiles, xs, wg, wu, wd):
    max_tiles = xs.shape[0] // EXPERT_TILE
    row = lambda i, te, nt: (jnp.minimum(i, nt[0] - 1), 0)
    out_row = lambda i, te, nt: (i, 0)
    wsel = lambda i, te, nt: (te[i], 0, 0)
    return pl.pallas_call(
        _expert_kernel,
        grid_spec=pltpu.PrefetchScalarGridSpec(
            num_scalar_prefetch=2,
            grid=(max_tiles,),
            in_specs=[pl.BlockSpec((EXPERT_TILE, D_MODEL), row),
                      pl.BlockSpec((1, D_MODEL, D_EXPERT), wsel),
                      pl.BlockSpec((1, D_MODEL, D_EXPERT), wsel),
                      pl.BlockSpec((1, D_EXPERT, D_MODEL), wsel)],
            out_specs=pl.BlockSpec((EXPERT_TILE, D_MODEL), out_row),
            scratch_shapes=[pltpu.VMEM((D_MODEL, 2 * D_EXPERT), BF16),
                            pltpu.VMEM((D_EXPERT, D_MODEL), BF16)]),
        out_shape=jax.ShapeDtypeStruct(xs.shape, F32),
        compiler_params=_params(("arbitrary",)),
        name="moe_experts",
    )(tile_expert, n_tiles, xs, wg, wu, wd)


def _combine_kernel(slot_ref, slot_next_ref, y_ref, rt_ref, x1_ref, mod_ref, gf_ref, o_ref, ybuf, sems, *, tmc):
    i = pl.program_id(0)
    n = pl.num_programs(0)

    def fetch(slots, half):
        def issue(r, carry):
            for k in range(2):
                _row_copy(y_ref, slots[0, 0, 2 * r + k], ybuf.at[half, k], r, sems.at[half]).start(priority=k)
            return carry

        lax.fori_loop(0, tmc, issue, 0, unroll=4)

    @pl.when(i == 0)
    def _():
        fetch(slot_ref, 0)

    @pl.when(i + 1 < n)
    def _():
        fetch(slot_next_ref, (i + 1) % 2)

    cur = i % 2

    def drain(r, carry):
        _row_copy(y_ref, 0, ybuf.at[cur, 0], 0, sems.at[cur]).wait()
        return carry

    lax.fori_loop(0, 2 * tmc, drain, 0, unroll=8)
    moe = rt_ref[:, 2:3] * ybuf[cur, 0] + rt_ref[:, 3:4] * ybuf[cur, 1]
    v = x1_ref[...] + mod_ref[0, 5:6, :] * moe
    o_ref[...] = v * lax.rsqrt(jnp.mean(v * v, axis=-1, keepdims=True) + EPS) * gf_ref[...]


def _combine(slots3, y, rtm, x1, mod3, gf, seq, tmc):
    t = x1.shape[0]
    per_b = seq // tmc
    n_steps = t // tmc
    tok = lambda w: pl.BlockSpec((tmc, w), lambda i: (i, 0))
    return pl.pallas_call(
        functools.partial(_combine_kernel, tmc=tmc),
        grid=(n_steps,),
        in_specs=[pl.BlockSpec((1, 1, 2 * tmc), lambda i: (i, 0, 0), memory_space=pltpu.SMEM),
                  pl.BlockSpec((1, 1, 2 * tmc), lambda i: (jnp.minimum(i + 1, n_steps - 1), 0, 0),
                               memory_space=pltpu.SMEM),
                  pl.BlockSpec(memory_space=pl.ANY),
                  tok(LANES), tok(D_MODEL),
                  pl.BlockSpec((1, N_MOD, D_MODEL), lambda i: (i // per_b, 0, 0)),
                  pl.BlockSpec((1, D_MODEL), lambda i: (0, 0))],
        out_specs=tok(D_MODEL),
        out_shape=jax.ShapeDtypeStruct((t, D_MODEL), F32),
        scratch_shapes=[pltpu.VMEM((2, 2, tmc, D_MODEL), F32), pltpu.SemaphoreType.DMA((2,))],
        compiler_params=_params(("arbitrary",)),
        name="moe_combine_final",
    )(slots3, slots3, y, rtm, x1, mod3, gf)


def _plan_slots(route_t, counts, n_tokens):
    e = route_t[0:2].astype(jnp.int32)
    rank = route_t[4:6].astype(jnp.int32)
    cnt = counts[:, 0].astype(jnp.int32)
    tiles = (cnt + EXPERT_TILE - 1) // EXPERT_TILE
    ids = jnp.arange(N_EXPERTS, dtype=jnp.int32)
    tile_end = jnp.sum(jnp.where(ids[:, None] <= ids[None, :], tiles[:, None], 0), axis=0)
    base = (tile_end - tiles) * EXPERT_TILE
    base_of = jnp.sum(jnp.where(e[:, :, None] == ids[None, None, :], base[None, None, :], 0), axis=-1)
    slots = (base_of + rank).T
    max_tiles = (2 * n_tokens + N_EXPERTS * (EXPERT_TILE - 1)) // EXPERT_TILE
    tile_ids = jnp.arange(max_tiles, dtype=jnp.int32)
    tile_expert = jnp.minimum(jnp.sum((tile_ids[:, None] >= tile_end[None, :]).astype(jnp.int32), axis=1),
                              N_EXPERTS - 1)
    last_tiles = jnp.where(tiles > 0, (tile_end - 1) * EXPERT_TILE, -1)
    spare = tile_end[-1] + jnp.arange(max_tiles - 2 * n_tokens // EXPERT_TILE, dtype=jnp.int32)
    spare_tiles = jnp.where(spare < max_tiles, spare * EXPERT_TILE, -1)
    zstart = jnp.concatenate([last_tiles, spare_tiles]).astype(jnp.int32)
    return slots, tile_expert, tile_end[-1:].astype(jnp.int32), zstart, max_tiles * EXPERT_TILE


def _rope_tables(seq, dim):
    inv = 1.0 / (ROPE_THETA ** (jnp.arange(0, dim, 2, dtype=F32) / dim))
    ang = jnp.arange(seq, dtype=F32)[:, None] * inv[None, :]
    return jnp.cos(ang), jnp.sin(ang)


def _tables(seq):
    ca, sa = _rope_tables(seq, SWA_HEAD_DIM)
    cb, sb = _rope_tables(seq, MLA_ROPE)
    z32 = jnp.zeros_like(sa)
    z16 = jnp.zeros_like(sb)
    z64 = jnp.zeros((seq, 64), F32)
    one64 = jnp.ones((seq, 64), F32)
    cat = lambda parts: jnp.concatenate(parts, axis=1)
    scale = float((MLA_NOPE + MLA_ROPE) ** -0.5 * np.log2(np.e))
    swa = [cat([ca] * 4), cat([z32, sa] * 2), cat([-sa, z32] * 2)]
    mq = [cat([one64, cb, cb, z32]) * scale, cat([z64, z16, sb, z32]) * scale, cat([z64, -sb, z16, z32]) * scale]
    mk = [cat([z64, cb, cb, z32]), cat([z64, z16, sb, z32]), cat([z64, -sb, z16, z32])]
    return jnp.stack(swa + mq + mk, axis=0)


def _layout_weights(w_in, w_uq, w_ukv, g_q):
    qa, ka, va, cq, ckv, kr = jnp.split(w_in, np.cumsum([512, 128, 128, 192, 128, 32])[:-1].tolist(), axis=1)
    zc = lambda n: jnp.zeros((D_MODEL, n), F32)
    w_in_l = jnp.concatenate([qa * (SWA_HEAD_DIM ** -0.5), ka, va, ckv, cq, zc(64), zc(64), kr, zc(32)], axis=1)
    uq = w_uq.reshape(MLA_Q_RANK, MLA_HEADS, MLA_NOPE + MLA_ROPE)
    uq = jnp.pad(uq, ((0, MLA_Q_RANK_PAD - MLA_Q_RANK), (0, 0), (0, LANES - MLA_NOPE - MLA_ROPE)))
    ukv = w_ukv.reshape(MLA_KV_RANK, MLA_HEADS, MLA_NOPE + MLA_V)
    wk = jnp.pad(ukv[:, :, :MLA_NOPE], ((0, 0), (0, 0), (0, LANES - MLA_NOPE)))
    wvt = ukv[:, :, MLA_NOPE:].reshape(MLA_KV_RANK, MLA_WIDTH).T
    gq = jnp.pad(g_q, (0, MLA_Q_RANK_PAD - MLA_Q_RANK))[None, :]
    return (w_in_l.astype(BF16), uq.reshape(MLA_Q_RANK_PAD, MLA_HEADS * LANES).astype(BF16),
            wk.reshape(MLA_KV_RANK, MLA_HEADS * LANES).astype(BF16), wvt.astype(BF16), gq)


def _layout_router(w_rg, b_rg, w_re, b_re):
    w = jnp.zeros((ROUTE_ROWS, D_MODEL), F32)
    w = w.at[0:N_GROUPS].set(w_rg.T).at[8:8 + N_EXPERTS].set(w_re.T)
    bias = jnp.zeros((ROUTE_ROWS,), F32).at[0:N_GROUPS].set(b_rg).at[8:8 + N_EXPERTS].set(b_re)
    hi = w.astype(BF16)
    lo = (w - hi.astype(F32)).astype(BF16)
    return jnp.concatenate([hi, lo], axis=0), bias[:, None]


def kernel(x, c, w_ada, b_ada, g_norm1, w_in, g_q_lora, w_uq, g_kv_lora, w_ukv, sink, g_out_swa, g_out_mla, w_out, g_norm2, w_router_group, b_router_group, w_router_expert, b_router_expert, w_exp_gate, w_exp_up, w_exp_down, g_final):
    b, s, d = x.shape
    assert d == D_MODEL and s % 512 == 0 and b <= 8
    assert w_ada.shape[0] == 1, "single layer"

    c8 = jnp.pad(c, ((0, 8 - b), (0, 0)))
    mod3 = _ada(c8, w_ada[0], b_ada[0][None, :]).reshape(8, N_MOD, D_MODEL)

    w_in_l, w_uq_l, w_k_l, w_vt_l, gq = _layout_weights(w_in[0], w_uq[0], w_ukv[0], g_q_lora[0])
    qa, ka, va, qm, km, vt = _proj(x, mod3, g_norm1[0][None, :], _tables(s), w_in_l, gq,
                                   g_kv_lora[0][None, :], w_uq_l, w_k_l, w_vt_l, tm=256)

    mixa = _swa(sink[0][None, :], qa, ka, va, g_out_swa[0][None, :])
    ob = _mla(qm, km, vt, tq=256, tk=256)

    wr, br = _layout_router(w_router_group[0], b_router_group[0], w_router_expert[0], b_router_expert[0])
    x1, h2, route_t, rtm, counts = _outproj(mixa, ob, x, mod3, g_out_mla[0][None, :], w_out[0].astype(BF16),
                                            g_norm2[0][None, :], wr, br, tm=256)

    t = b * s
    slots, tile_expert, n_tiles, zstart, n_slots = _plan_slots(route_t, counts, t)
    tmd, tmc = 512, 256
    xs = _dispatch(slots.reshape(t // tmd, 1, 2 * tmd), zstart, h2.reshape(t, d), n_slots, tmd)
    y = _experts(tile_expert, n_tiles, xs, w_exp_gate[0], w_exp_up[0], w_exp_down[0])
    out = _combine(slots.reshape(t // tmc, 1, 2 * tmc), y, rtm, x1.reshape(t, d), mod3, g_final[None, :], s, tmc)
    return out.reshape(b, s, d)
```
